```python
import math
import jax, jax.numpy as jnp
from jax import lax
import numpy as np

D_MODEL = 1024
BATCH = 4
SEQ = 8192
DEPTH = 2

DIFF_HEADS = 4
DIFF_QK_DIM = 32
DIFF_V_DIM = 2 * DIFF_QK_DIM
ATTN_BLOCK = 128
DIL_HEADS = 6
DIL_HEAD_DIM = 64
DIL_PATTERNS = ((128, 1), (512, 4), (2048, 16))
DIL_BLOCK = 128
SSD_HEADS = 6
SSD_HEAD_DIM = 64
SSD_GROUPS = 2
SSD_HEADS_PER_GROUP = SSD_HEADS // SSD_GROUPS
SSD_STATE = 128
SSD_CONV = 4
SSD_CHUNK = 128
SSD_INNER = SSD_HEADS * SSD_HEAD_DIM
SSD_XBC = SSD_INNER + 2 * SSD_GROUPS * SSD_STATE
DIFF_WIDTH = DIFF_HEADS * DIFF_V_DIM
DIL_WIDTH = DIL_HEADS * DIL_HEAD_DIM
MIX_WIDTH = DIFF_WIDTH + DIL_WIDTH + SSD_INNER
IN_SIZES = (DIFF_HEADS * 2 * DIFF_QK_DIM, DIFF_HEADS * 2 * DIFF_QK_DIM, DIFF_WIDTH,
            DIL_WIDTH, DIL_WIDTH, DIL_WIDTH,
            SSD_INNER, SSD_XBC, SSD_HEADS)
IN_WIDTH = sum(IN_SIZES)
IN_OFFSETS = tuple(sum(IN_SIZES[:i + 1]) for i in range(len(IN_SIZES) - 1))
D_FF = 2816
FFN_CONV = 3
ROPE_THETA = 10000.0
NORM_EPS = 1e-6

kernel_name = "hymba_style_diff_dilated_ssd_hybrid"


def rms_norm(x, gain, eps=NORM_EPS):
    xf = x.astype(jnp.float32)
    y = xf * lax.rsqrt(jnp.mean(xf * xf, axis=-1, keepdims=True) + eps)
    return (y * gain.astype(jnp.float32)).astype(x.dtype)


def rope(x, pos):
    dim = x.shape[-1]
    half = dim // 2
    inv_freq = jnp.exp(-math.log(ROPE_THETA) * jnp.arange(half, dtype=jnp.float32) / half)
    ang = pos.astype(jnp.float32)[:, None] * inv_freq[None, :]
    shape = (1, pos.shape[0]) + (1,) * (x.ndim - 3) + (half,)
    cos = jnp.cos(ang).reshape(shape)
    sin = jnp.sin(ang).reshape(shape)
    xf = x.astype(jnp.float32)
    x1, x2 = xf[..., :half], xf[..., half:]
    out = jnp.concatenate([x1 * cos - x2 * sin, x2 * cos + x1 * sin], axis=-1)
    return out.astype(x.dtype)


def causal_dwconv(x, w, b):
    K, C = w.shape
    y = lax.conv_general_dilated(
        x, w[:, None, :].astype(x.dtype), window_strides=(1,), padding=[(K - 1, 0)],
        dimension_numbers=('NWC', 'WIO', 'NWC'), feature_group_count=C)
    return y + b.astype(x.dtype)


def diff_attention(q, k, v, lam, head_gain, lambda_init):
    Bn, S, H, _, dq = q.shape
    dv = v.shape[-1]
    nblk = S // ATTN_BLOCK
    scale = dq ** -0.5
    qb = q.reshape(Bn, nblk, ATTN_BLOCK, H, 2, dq).transpose(1, 0, 2, 3, 4, 5)
    kpos = jnp.arange(S)

    def block(args):
        qblk, start = args
        s = jnp.einsum('bqhmd,bkhmd->bhmqk', qblk, k,
                       preferred_element_type=jnp.float32) * scale
        qpos = start + jnp.arange(ATTN_BLOCK)
        mask = kpos[None, :] <= qpos[:, None]
        p = jax.nn.softmax(jnp.where(mask, s, -jnp.inf), axis=-1)
        a = (p[:, :, 0] - lam * p[:, :, 1]).astype(v.dtype)
        return jnp.einsum('bhqk,bkhe->bqhe', a, v)

    starts = jnp.arange(nblk) * ATTN_BLOCK
    o = lax.map(block, (qb, starts))
    o = o.transpose(1, 0, 2, 3, 4).reshape(Bn, S, H, dv)
    o = rms_norm(o, head_gain) * (1.0 - lambda_init)
    return o.reshape(Bn, S, H * dv)


def dilated_pattern(q, k, v, window, dilation):
    Bn, S, H, Dh = q.shape
    span = window // dilation
    unit = dilation * DIL_BLOCK
    L = -(-S // unit) * unit
    M = L // dilation
    nb = M // DIL_BLOCK

    def to_blocks(t):
        t = jnp.pad(t, ((0, 0), (0, L - S), (0, 0), (0, 0)))
        t = t.reshape(Bn, M, dilation, H, Dh).transpose(0, 2, 3, 1, 4)
        return t.reshape(Bn, dilation, H, nb, DIL_BLOCK, Dh)

    def with_prev(t):
        prev = jnp.pad(t, ((0, 0), (0, 0), (0, 0), (1, 0), (0, 0), (0, 0)))[:, :, :, :-1]
        return jnp.concatenate([prev, t], axis=4)

    qb = to_blocks(q)
    kk = with_prev(to_blocks(k))
    vv = with_prev(to_blocks(v))
    s = jnp.einsum('brhnqd,brhnkd->brhnqk', qb, kk,
                   preferred_element_type=jnp.float32) * (Dh ** -0.5)
    qi = jnp.arange(nb)[:, None, None] * DIL_BLOCK + jnp.arange(DIL_BLOCK)[None, :, None]
    kj = (jnp.arange(nb)[:, None, None] - 1) * DIL_BLOCK + jnp.arange(2 * DIL_BLOCK)[None, None, :]
    dist = qi - kj
    mask = (kj >= 0) & (dist >= 0) & (dist <= span)
    s = jnp.where(mask, s, -jnp.inf)
    m = jnp.max(s, axis=-1, keepdims=True)
    p = jnp.exp(s - m)
    den = jnp.sum(p, axis=-1, keepdims=True)
    o = jnp.einsum('brhnqk,brhnkd->brhnqd', (p / den).astype(v.dtype), vv)
    lse = (m + jnp.log(den))[..., 0]
    o = o.reshape(Bn, dilation, H, M, Dh).transpose(0, 3, 1, 2, 4).reshape(Bn, L, H, Dh)[:, :S]
    lse = lse.reshape(Bn, dilation, H, M).transpose(0, 3, 1, 2).reshape(Bn, L, H)[:, :S]
    return o, lse


def dilated_attention(q, k, v):
    Bn, S, H, Dh = q.shape
    res = [dilated_pattern(q, k, v, w, d) for (w, d) in DIL_PATTERNS]
    outs = jnp.stack([r[0] for r in res]).astype(jnp.float32)
    wts = jax.nn.softmax(jnp.stack([r[1] for r in res]), axis=0)
    o = jnp.sum(wts[..., None] * outs, axis=0)
    return o.astype(q.dtype).reshape(Bn, S, H * Dh)


def segsum(a):
    T = a.shape[-1]
    cs = jnp.cumsum(a, axis=-1)
    diff = cs[..., :, None] - cs[..., None, :]
    mask = jnp.tril(jnp.ones((T, T), dtype=bool))
    return jnp.where(mask, diff, -jnp.inf)


def ssd_mixer(z, xbc, dt, conv_w, conv_b, dt_bias, A_log, D, norm_gain):
    Bn, S, _ = z.shape
    G, HG, P, N = SSD_GROUPS, SSD_HEADS_PER_GROUP, SSD_HEAD_DIM, SSD_STATE
    nc, l = S // SSD_CHUNK, SSD_CHUNK
    xbc = jax.nn.silu(causal_dwconv(xbc, conv_w, conv_b))
    xs, Bs, Cs = jnp.split(xbc, [SSD_INNER, SSD_INNER + G * N], axis=-1)
    x = xs.reshape(Bn, nc, l, G, HG, P).astype(jnp.float32)
    Bm = Bs.reshape(Bn, nc, l, G, N).astype(jnp.float32)
    Cm = Cs.reshape(Bn, nc, l, G, N).astype(jnp.float32)
    dt = jax.nn.softplus(dt.astype(jnp.float32) + dt_bias.astype(jnp.float32))
    A = -jnp.exp(A_log.astype(jnp.float32))
    dt_c = dt.reshape(Bn, nc, l, G, HG)
    a = (dt * A).reshape(Bn, nc, l, G, HG).transpose(0, 3, 4, 1, 2)
    a_cs = jnp.cumsum(a, axis=-1)
    Lmat = jnp.exp(segsum(a))
    xdt = x * dt_c[..., None]
    cb = jnp.einsum('bclgn,bcsgn->bcgls', Cm, Bm)
    y_diag = jnp.einsum('bcgls,bghcls,bcsghp->bclghp', cb, Lmat, xdt)
    decay_states = jnp.exp(a_cs[..., -1:] - a_cs)
    states = jnp.einsum('bcsgn,bghcs,bcsghp->cbghpn', Bm, decay_states, xdt)
    chunk_decay = jnp.exp(a_cs[..., -1]).transpose(3, 0, 1, 2)

    def step(h, inp):
        s_c, dec = inp
        return dec[..., None, None] * h + s_c, h

    _, prev = lax.scan(step, jnp.zeros_like(states[0]), (states, chunk_decay))
    y_off = jnp.einsum('bclgn,cbghpn,bghcl->bclghp', Cm, prev, jnp.exp(a_cs))
    y = y_diag + y_off + x * D.astype(jnp.float32).reshape(G, HG, 1)
    y = y.reshape(Bn, S, SSD_INNER) * jax.nn.silu(z.astype(jnp.float32))
    yg = y.reshape(Bn, S, G, SSD_INNER // G)
    yg = yg * lax.rsqrt(jnp.mean(yg * yg, axis=-1, keepdims=True) + NORM_EPS)
    out = yg.reshape(Bn, S, SSD_INNER) * norm_gain.astype(jnp.float32)
    return out.astype(z.dtype)


def setup_inputs(seed: int = 0) -> dict:
    key = jax.random.key(seed)
    ks = jax.random.split(key, 24)
    f32 = jnp.float32

    def gain(k, n):
        return 1.0 + 0.01 * jax.random.normal(k, (DEPTH, n), f32)

    dt0 = jnp.exp(jax.random.uniform(ks[8], (DEPTH, SSD_HEADS), f32,
                                     math.log(0.001), math.log(0.1)))
    return {
        "x": jax.random.normal(ks[0], (BATCH, SEQ, D_MODEL), f32),
        "pre_mix_norm": gain(ks[1], D_MODEL),
        "w_in": jax.random.normal(ks[2], (DEPTH, D_MODEL, IN_WIDTH), f32) * D_MODEL ** -0.5,
        "diff_lambda": 0.1 * jax.random.normal(ks[3], (DEPTH, 4, DIFF_QK_DIM), f32),
        "diff_head_norm": gain(ks[4], DIFF_V_DIM),
        "ssd_conv_w": jax.random.normal(ks[5], (DEPTH, SSD_CONV, SSD_XBC), f32) * SSD_CONV ** -0.5,
        "ssd_conv_b": 0.01 * jax.random.normal(ks[6], (DEPTH, SSD_XBC), f32),
        "ssd_dt_bias": dt0 + jnp.log(-jnp.expm1(-dt0)),
        "ssd_A_log": jnp.log(jax.random.uniform(ks[9], (DEPTH, SSD_HEADS), f32, 1.0, 16.0)),
        "ssd_D": 1.0 + 0.1 * jax.random.normal(ks[10], (DEPTH, SSD_HEADS), f32),
        "ssd_norm": gain(ks[11], SSD_INNER),
        "w_out": jax.random.normal(ks[12], (DEPTH, MIX_WIDTH, D_MODEL), f32) * MIX_WIDTH ** -0.5,
        "post_mix_norm": gain(ks[13], D_MODEL),
        "pre_ffn_norm": gain(ks[14], D_MODEL),
        "ffn_up": jax.random.normal(ks[15], (DEPTH, D_MODEL, 2 * D_FF), f32) * D_MODEL ** -0.5,
        "ffn_conv_w": jax.random.normal(ks[16], (DEPTH, FFN_CONV, 2 * D_FF), f32) * FFN_CONV ** -0.5,
        "ffn_conv_b": 0.01 * jax.random.normal(ks[17], (DEPTH, 2 * D_FF), f32),
        "ffn_down": jax.random.normal(ks[18], (DEPTH, D_FF, D_MODEL), f32) * D_FF ** -0.5,
        "post_ffn_norm": gain(ks[19], D_MODEL),
    }


def reference(x, pre_mix_norm, w_in, diff_lambda, diff_head_norm, ssd_conv_w, ssd_conv_b,
              ssd_dt_bias, ssd_A_log, ssd_D, ssd_norm, w_out, post_mix_norm, pre_ffn_norm,
              ffn_up, ffn_conv_w, ffn_conv_b, ffn_down, post_ffn_norm):
    Bn, S, _ = x.shape
    pos = jnp.arange(S, dtype=jnp.int32)
    for layer in range(DEPTH):
        lambda_init = 0.8 - 0.6 * math.exp(-0.3 * layer)
        h = rms_norm(x, pre_mix_norm[layer])
        proj = h @ w_in[layer]
        dq, dk, dv, lq, lk, lv, z, xbc, dt = jnp.split(proj, IN_OFFSETS, axis=-1)
        dq = rope(dq.reshape(Bn, S, DIFF_HEADS, 2, DIFF_QK_DIM), pos)
        dk = rope(dk.reshape(Bn, S, DIFF_HEADS, 2, DIFF_QK_DIM), pos)
        dv = dv.reshape(Bn, S, DIFF_HEADS, DIFF_V_DIM)
        lam_p = diff_lambda[layer].astype(jnp.float32)
        lam = (jnp.exp(jnp.sum(lam_p[0] * lam_p[1])) - jnp.exp(jnp.sum(lam_p[2] * lam_p[3]))
               + lambda_init)
        o_diff = diff_attention(dq, dk, dv, lam, diff_head_norm[layer], lambda_init)
        lq = rope(lq.reshape(Bn, S, DIL_HEADS, DIL_HEAD_DIM), pos)
        lk = rope(lk.reshape(Bn, S, DIL_HEADS, DIL_HEAD_DIM), pos)
        lv = lv.reshape(Bn, S, DIL_HEADS, DIL_HEAD_DIM)
        o_dil = dilated_attention(lq, lk, lv)
        o_ssd = ssd_mixer(z, xbc, dt, ssd_conv_w[layer], ssd_conv_b[layer], ssd_dt_bias[layer],
                          ssd_A_log[layer], ssd_D[layer], ssd_norm[layer])
        mix = jnp.concatenate([o_diff.astype(x.dtype), o_dil.astype(x.dtype),
                               o_ssd.astype(x.dtype)], axis=-1) @ w_out[layer]
        x = x + rms_norm(mix, post_mix_norm[layer])
        h = rms_norm(x, pre_ffn_norm[layer])
        u = causal_dwconv(h @ ffn_up[layer], ffn_conv_w[layer], ffn_conv_b[layer])
        g, val = jnp.split(u, [D_FF], axis=-1)
        f = (jax.nn.silu(g) * val) @ ffn_down[layer]
        x = x + rms_norm(f, post_ffn_norm[layer])
    return x
```

```python
import functools
import math

import jax
import jax.numpy as jnp
from jax import lax
from jax.experimental import pallas as pl
from jax.experimental.pallas import tpu as pltpu

F32 = jnp.float32
BF16 = jnp.bfloat16

LANES = 128
SUBLANES = 8
VMEM_LIMIT = 56 * 1024 * 1024

D_MODEL = 1024
DIFF_HEADS = 4
DIFF_QK = 32
DIFF_V = 64
DIFF_W = DIFF_HEADS * DIFF_V
DIL_HEADS = 6
DIL_DH = 64
DIL_W = DIL_HEADS * DIL_DH
DIL_PATTERNS = ((128, 1), (512, 4), (2048, 16))
DIL_BLOCK = 128
SSD_HEADS = 6
SSD_P = 64
SSD_GROUPS = 2
SSD_N = 128
SSD_CONV = 4
SSD_CHUNK = 128
SSD_INNER = SSD_HEADS * SSD_P
SSD_XBC = SSD_INNER + 2 * SSD_GROUPS * SSD_N
D_FF = 2816
FFN_CONV = 3
ROPE_THETA = 10000.0
EPS = 1e-6

IN_SIZES = (DIFF_W, DIFF_W, DIFF_W, DIL_W, DIL_W, DIL_W, SSD_INNER, SSD_XBC, SSD_HEADS)

TM_IN = 512
BQ = 256
DIL_CH = 2048
SSD_LC = 512
TM_FFN = 256
FF_CHUNK = 256


def _cparams(sem):
    return pltpu.CompilerParams(dimension_semantics=sem, vmem_limit_bytes=VMEM_LIMIT)


def _const_spec(shape):
    nd = len(shape)
    return pl.BlockSpec(shape, lambda *_: (0,) * nd)


def _silu(x):
    return x * (1.0 / (1.0 + jnp.exp(-x)))


def _split3(x):
    hi = x.astype(BF16)
    r = x - hi.astype(F32)
    mid = r.astype(BF16)
    lo = (r - mid.astype(F32)).astype(BF16)
    return hi, mid, lo


def _rope_lanes(y, cos, sin_up, sin_dn, half):
    outs = []
    for j in range(y.shape[1] // LANES):
        yc = y[:, j * LANES:(j + 1) * LANES]
        up = pltpu.roll(yc, LANES - half, 1)
        dn = pltpu.roll(yc, half, 1)
        outs.append(yc * cos + up * sin_up + dn * sin_dn)
    return jnp.concatenate(outs, axis=1)


def _in_proj_kernel(x_ref, g_ref, cd_ref, sud_ref, sdd_ref, cl_ref, sul_ref, sdl_ref,
                    wq_ref, wk_ref, wv_ref, wlq_ref, wlk_ref, wlv_ref, wz_ref, wxbc_ref, wdt_ref,
                    oq_ref, ok_ref, ov_ref, olq_ref, olk_ref, olv_ref, oz_ref, oxbc_ref, odt_ref):
    x = x_ref[...]
    ms = jnp.mean(x * x, axis=-1, keepdims=True)
    h = ((x * lax.rsqrt(ms + EPS)) * g_ref[...]).astype(BF16)

    def proj(w_ref):
        return jnp.dot(h, w_ref[...], preferred_element_type=F32)

    cd, sud, sdd = cd_ref[...], sud_ref[...], sdd_ref[...]
    q = _rope_lanes(proj(wq_ref), cd, sud, sdd, DIFF_QK // 2)
    oq_ref[...] = (q * (DIFF_QK ** -0.5)).astype(BF16)
    ok_ref[...] = _rope_lanes(proj(wk_ref), cd, sud, sdd, DIFF_QK // 2).astype(BF16)
    ov_ref[...] = proj(wv_ref).astype(BF16)
    cl, sul, sdl = cl_ref[...], sul_ref[...], sdl_ref[...]
    lq = _rope_lanes(proj(wlq_ref), cl, sul, sdl, DIL_DH // 2)
    olq_ref[...] = (lq * (DIL_DH ** -0.5)).astype(BF16)
    olk_ref[...] = _rope_lanes(proj(wlk_ref), cl, sul, sdl, DIL_DH // 2).astype(BF16)
    olv_ref[...] = proj(wlv_ref).astype(BF16)
    oz_ref[...] = proj(wz_ref).astype(BF16)
    oxbc_ref[...] = proj(wxbc_ref)
    odt_ref[...] = proj(wdt_ref)


def _in_proj(x, gain, tables, ws):
    B, S, D = x.shape
    tm = TM_IN
    grid = (S // tm, B)
    tok = lambda w: pl.BlockSpec((None, tm, w), lambda s, b: (b, s, 0))
    tab = pl.BlockSpec((tm, LANES), lambda s, b: (s, 0))
    out_widths = (DIFF_W, DIFF_W, DIFF_W, DIL_W, DIL_W, DIL_W, SSD_INNER, SSD_XBC, LANES)
    out_dtypes = (BF16, BF16, BF16, BF16, BF16, BF16, BF16, F32, F32)
    return pl.pallas_call(
        _in_proj_kernel,
        grid=grid,
        in_specs=[tok(D), _const_spec((1, D))] + [tab] * 6 + [_const_spec(w.shape) for w in ws],
        out_specs=[tok(w) for w in out_widths],
        out_shape=[jax.ShapeDtypeStruct((B, S, w), dt) for w, dt in zip(out_widths, out_dtypes)],
        compiler_params=_cparams(("arbitrary", "arbitrary")),
        name="in_proj",
    )(x, gain, *tables, *ws)


N_SM = 2 * DIFF_HEADS


def _diff_attn_kernel(lam_ref, gain_ref, gmat_ref, q_ref, k_ref, v_ref, o_ref,
                      qs_ref, m_ref, l_ref, acc_ref, *, lambda_init):
    i = pl.program_id(1)
    bq = q_ref.shape[0]
    q = q_ref[...]
    lane = lax.broadcasted_iota(jnp.int32, (bq, DIFF_W), 1)
    zero = jnp.zeros_like(q)
    for g in range(N_SM):
        qs_ref[g * bq:(g + 1) * bq, :] = jnp.where(lane // DIFF_QK == g, q, zero)
    m_ref[...] = jnp.full(m_ref.shape, -jnp.inf, F32)
    l_ref[...] = jnp.zeros(l_ref.shape, F32)
    acc_ref[...] = jnp.zeros(acc_ref.shape, F32)

    def step(j, masked):
        start = pl.multiple_of(j * bq, bq)
        kj = k_ref[pl.ds(start, bq), :]
        vj = v_ref[pl.ds(start, bq), :]
        s = lax.dot_general(qs_ref[...], kj, (((1,), (1,)), ((), ())),
                            preferred_element_type=F32)
        if masked:
            row = lax.broadcasted_iota(jnp.int32, (bq, bq), 0)
            col = lax.broadcasted_iota(jnp.int32, (bq, bq), 1)
            s = jnp.where((col <= row)[None], s.reshape(N_SM, bq, bq), -jnp.inf).reshape(N_SM * bq, bq)
        m_prev = m_ref[...]
        m_new = jnp.maximum(m_prev, jnp.max(s, axis=1, keepdims=True))
        alpha = jnp.exp(m_prev - m_new)
        p = jnp.exp(s - pltpu.repeat(m_new, bq // LANES, axis=1))
        l_ref[...] = alpha * l_ref[...] + jnp.sum(p, axis=1, keepdims=True)
        pv = jnp.dot(p.astype(BF16), vj, preferred_element_type=F32)
        acc_ref[...] = acc_ref[...] * pltpu.repeat(alpha, DIFF_W // LANES, axis=1) + pv
        m_ref[...] = m_new

    def body(j, carry):
        step(j, False)
        return carry

    lax.fori_loop(0, i, body, 0)
    step(i, True)

    lam_p = lam_ref[...]
    s1 = jnp.sum(lam_p[0:1, :] * lam_p[1:2, :], axis=1, keepdims=True)
    s2 = jnp.sum(lam_p[2:3, :] * lam_p[3:4, :], axis=1, keepdims=True)
    lam = jnp.exp(s1) - jnp.exp(s2) + lambda_init
    inv_l = pltpu.repeat(1.0 / l_ref[...], DIFF_W // LANES, axis=1)
    o = acc_ref[...] * inv_l
    comb = jnp.zeros((bq, DIFF_W), F32)
    for hd in range(DIFF_HEADS):
        d = o[(2 * hd) * bq:(2 * hd + 1) * bq, :] - lam * o[(2 * hd + 1) * bq:(2 * hd + 2) * bq, :]
        comb = jnp.where(lane // DIFF_V == hd, d, comb)
    hi, mid, lo = _split3(comb * comb)
    gm = gmat_ref[...]
    ms = (jnp.dot(hi, gm, preferred_element_type=F32) + jnp.dot(mid, gm, preferred_element_type=F32)
          + jnp.dot(lo, gm, preferred_element_type=F32))
    y = (comb * lax.rsqrt(ms + EPS)) * gain_ref[...]
    o_ref[...] = (y * (1.0 - lambda_init)).astype(o_ref.dtype)


def _diff_attn(q, k, v, lam_p, gain_t, gmat, lambda_init):
    B, S, W = q.shape
    grid = (B, S // BQ)
    return pl.pallas_call(
        functools.partial(_diff_attn_kernel, lambda_init=lambda_init),
        grid=grid,
        in_specs=[_const_spec(lam_p.shape), _const_spec(gain_t.shape), _const_spec(gmat.shape),
                  pl.BlockSpec((None, BQ, W), lambda b, i: (b, i, 0)),
                  pl.BlockSpec((None, S, W), lambda b, i: (b, 0, 0)),
                  pl.BlockSpec((None, S, W), lambda b, i: (b, 0, 0))],
        out_specs=pl.BlockSpec((None, BQ, W), lambda b, i: (b, i, 0)),
        out_shape=jax.ShapeDtypeStruct((B, S, W), BF16),
        scratch_shapes=[pltpu.VMEM((N_SM * BQ, W), BF16),
                        pltpu.VMEM((N_SM * BQ, LANES), F32),
                        pltpu.VMEM((N_SM * BQ, LANES), F32),
                        pltpu.VMEM((N_SM * BQ, W), F32)],
        compiler_params=_cparams(("arbitrary", "arbitrary")),
        name="diff_attn",
    )(lam_p, gain_t, gmat, q, k, v)


N_PAIRS = DIL_W // LANES


def _dil_pair(qp, kp, vp, first_block):
    nq, nk = DIL_BLOCK, 2 * DIL_BLOCK
    iq = lax.broadcasted_iota(jnp.int32, (nq, nk), 0)
    jk = lax.broadcasted_iota(jnp.int32, (nq, nk), 1)
    band = (jk >= iq) & (jk <= iq + DIL_BLOCK)
    valid = band & (jk >= jnp.where(first_block, DIL_BLOCK, 0))
    low = lax.broadcasted_iota(jnp.int32, (nq, LANES), 1) < DIL_DH
    kb = kp.astype(BF16)
    vb = vp.astype(BF16)
    res = []
    for e in range(2):
        keep = low if e == 0 else jnp.logical_not(low)
        qm = jnp.where(keep, qp, 0.0).astype(BF16)
        s = lax.dot_general(qm, kb, (((1,), (1,)), ((), ())), preferred_element_type=F32)
        s = jnp.where(valid, s, -jnp.inf)
        m = jnp.max(s, axis=1, keepdims=True)
        p = jnp.exp(s - m)
        den = jnp.sum(p, axis=1, keepdims=True)
        num = jnp.dot(p.astype(BF16), vb, preferred_element_type=F32)
        res.append((num, den, m))
    return tuple(jnp.where(low, res[0][t], res[1][t]) for t in range(3))


def _dil_merge(a, b):
    (n1, d1, m1), (n2, d2, m2) = a, b
    m = jnp.maximum(m1, m2)
    w1 = jnp.exp(m1 - m)
    w2 = jnp.exp(m2 - m)
    return n1 * w1 + n2 * w2, d1 * w1 + d2 * w2, m


def _dil_attn_kernel(q_ref, k_ref, v_ref, o_ref, qf_ref, kk_ref, vv_ref, num_ref, den_ref, max_ref):
    c = pl.program_id(1)
    ch = q_ref.shape[0]

    @pl.when(c == 0)
    def _():
        kk_ref[:, 0:ch, :] = jnp.zeros((N_PAIRS, ch, LANES), F32)
        vv_ref[:, 0:ch, :] = jnp.zeros((N_PAIRS, ch, LANES), F32)

    for pr in range(N_PAIRS):
        sl = slice(pr * LANES, (pr + 1) * LANES)
        qf_ref[pr] = q_ref[:, sl].astype(F32)
        kk_ref[pr, ch:2 * ch, :] = k_ref[:, sl].astype(F32)
        vv_ref[pr, ch:2 * ch, :] = v_ref[:, sl].astype(F32)
    first_chunk = c == 0

    def block(pr, rows, keys, first_block):
        return _dil_pair(qf_ref[pr, rows, :], kk_ref[pr, keys, :], vv_ref[pr, keys, :], first_block)

    def merged(pr, rows, new):
        return _dil_merge((num_ref[pr, rows, :], den_ref[pr, rows, :], max_ref[pr, rows, :]), new)

    def store(pr, rows, ndm):
        num_ref[pr, rows, :] = ndm[0]
        den_ref[pr, rows, :] = ndm[1]
        max_ref[pr, rows, :] = ndm[2]

    d16 = DIL_PATTERNS[2][1]

    def body16(r, carry):
        rows = pl.ds(r, DIL_BLOCK, stride=d16)
        keys = pl.ds(r, 2 * DIL_BLOCK, stride=d16)
        for pr in range(N_PAIRS):
            store(pr, rows, block(pr, rows, keys, first_chunk))
        return carry

    lax.fori_loop(0, d16, body16, 0)

    d4 = DIL_PATTERNS[1][1]
    blocks4 = ch // (d4 * DIL_BLOCK)

    def body4(t, carry):
        r = t % d4
        nb = t // d4
        q0 = r + nb * (d4 * DIL_BLOCK)
        rows = pl.ds(q0, DIL_BLOCK, stride=d4)
        keys = pl.ds(ch + q0 - d4 * DIL_BLOCK, 2 * DIL_BLOCK, stride=d4)
        first_block = jnp.logical_and(first_chunk, nb == 0)
        for pr in range(N_PAIRS):
            store(pr, rows, merged(pr, rows, block(pr, rows, keys, first_block)))
        return carry

    lax.fori_loop(0, d4 * blocks4, body4, 0)

    def body1(nb, carry):
        q0 = pl.multiple_of(nb * DIL_BLOCK, DIL_BLOCK)
        rows = pl.ds(q0, DIL_BLOCK)
        keys = pl.ds(pl.multiple_of(ch + q0 - DIL_BLOCK, DIL_BLOCK), 2 * DIL_BLOCK)
        first_block = jnp.logical_and(first_chunk, nb == 0)
        for pr in range(N_PAIRS):
            n, d, _ = merged(pr, rows, block(pr, rows, keys, first_block))
            o_ref[rows, pr * LANES:(pr + 1) * LANES] = (n / d).astype(o_ref.dtype)
        return carry

    lax.fori_loop(0, ch // DIL_BLOCK, body1, 0)

    kk_ref[:, 0:ch, :] = kk_ref[:, ch:2 * ch, :]
    vv_ref[:, 0:ch, :] = vv_ref[:, ch:2 * ch, :]


def _dil_attn(q, k, v):
    B, S, W = q.shape
    ch = DIL_CH
    blk = pl.BlockSpec((None, ch, W), lambda b, c: (b, c, 0))
    return pl.pallas_call(
        _dil_attn_kernel,
        grid=(B, S // ch),
        in_specs=[blk, blk, blk],
        out_specs=blk,
        out_shape=jax.ShapeDtypeStruct((B, S, W), BF16),
        scratch_shapes=[pltpu.VMEM((N_PAIRS, ch, LANES), F32), pltpu.VMEM((N_PAIRS, 2 * ch, LANES), F32),
                        pltpu.VMEM((N_PAIRS, 2 * ch, LANES), F32), pltpu.VMEM((N_PAIRS, ch, LANES), F32),
                        pltpu.VMEM((N_PAIRS, ch, LANES), F32), pltpu.VMEM((N_PAIRS, ch, LANES), F32)],
        compiler_params=_cparams(("arbitrary", "arbitrary")),
        name="dil_attn",
    )(q, k, v)


def _shift_rows(x, tail8, s):
    n = x.shape[0]
    row = lax.broadcasted_iota(jnp.int32, x.shape, 0)
    head = pltpu.repeat(pltpu.roll(tail8, s, 0), n // SUBLANES, axis=0)
    return jnp.where(row < s, head, pltpu.roll(x, s, 0))


def _expand_heads(v, n_heads, width):
    rows = v.shape[0]
    lane = lax.broadcasted_iota(jnp.int32, (rows, n_heads * width), 1)
    out = jnp.zeros((rows, n_heads * width), F32)
    for h in range(n_heads):
        out = jnp.where(lane // width == h, v[:, h:h + 1], out)
    return out


def _ssd_kernel(z_ref, xbc_ref, dt_ref, cw_ref, cb_ref, dtb_ref, alog_ref, dvec_ref, gain_ref, tri_ref,
                o_ref, state_ref, tail_ref):
    c = pl.program_id(1)
    lc = xbc_ref.shape[0]
    L = SSD_CHUNK

    @pl.when(c == 0)
    def _():
        state_ref[...] = jnp.zeros(state_ref.shape, F32)
        tail_ref[...] = jnp.zeros(tail_ref.shape, F32)

    xbc = xbc_ref[...]
    tail8 = tail_ref[...]
    cw = cw_ref[...]
    acc = xbc * cw[SSD_CONV - 1:SSD_CONV, :] + cb_ref[...]
    for kk in range(SSD_CONV - 1):
        acc = acc + _shift_rows(xbc, tail8, SSD_CONV - 1 - kk) * cw[kk:kk + 1, :]
    tail_ref[...] = xbc[lc - SUBLANES:lc, :]
    xbc_c = _silu(acc)

    xs_all = xbc_c[:, 0:SSD_INNER]
    dt_raw = dt_ref[...] + dtb_ref[...]
    dt_all = jnp.maximum(dt_raw, 0.0) + jnp.log(1.0 + jnp.exp(-jnp.abs(dt_raw)))
    a_all = dt_all * (-jnp.exp(alog_ref[...]))
    tri = tri_ref[...]
    lane_in = lax.broadcasted_iota(jnp.int32, (L, SSD_INNER), 1)
    grp_w = SSD_INNER // SSD_GROUPS
    ii = lax.broadcasted_iota(jnp.int32, (L, L), 0)
    jj = lax.broadcasted_iota(jnp.int32, (L, L), 1)
    causal = jj <= ii

    for ck in range(lc // L):
        rs = slice(ck * L, (ck + 1) * L)
        xs = xs_all[rs, :]
        a = a_all[rs, :]
        hi, mid, lo = _split3(a)
        acs = (jnp.dot(tri, hi, preferred_element_type=F32) + jnp.dot(tri, mid, preferred_element_type=F32)
               + jnp.dot(tri, lo, preferred_element_type=F32))
        acs_t = acs.T
        dt_e = _expand_heads(dt_all[rs, :], SSD_HEADS, SSD_P)
        acs_e = _expand_heads(acs, SSD_HEADS, SSD_P)
        last_e = acs_e[L - 1:L, :]
        xdt = xs * dt_e
        xdt_b = xdt.astype(BF16)
        xdtd_b = (xdt * jnp.exp(last_e - acs_e)).astype(BF16)
        prev = state_ref[...]
        prev_b = prev.astype(BF16)
        y = xs * dvec_ref[...]
        new_state = prev * jnp.exp(last_e)
        y_off = jnp.zeros((L, SSD_INNER), F32)
        for g in range(SSD_GROUPS):
            bg = xbc_c[rs, SSD_INNER + g * SSD_N:SSD_INNER + (g + 1) * SSD_N]
            cg = xbc_c[rs, SSD_INNER + (SSD_GROUPS + g) * SSD_N:SSD_INNER + (SSD_GROUPS + g + 1) * SSD_N]
            bg_b = bg.astype(BF16)
            cg_b = cg.astype(BF16)
            in_grp = lane_in // grp_w == g
            cb = lax.dot_general(cg_b, bg_b, (((1,), (1,)), ((), ())), preferred_element_type=F32)
            for hh in range(SSD_HEADS // SSD_GROUPS):
                hd = g * (SSD_HEADS // SSD_GROUPS) + hh
                seg = acs[:, hd:hd + 1] - acs_t[hd:hd + 1, :]
                w = jnp.where(causal, cb * jnp.exp(seg), 0.0).astype(BF16)
                y = y + jnp.dot(w, jnp.where(lane_in // SSD_P == hd, xdt_b, jnp.zeros_like(xdt_b)),
                                preferred_element_type=F32)
            zero_b = jnp.zeros_like(xdtd_b)
            new_state = new_state + jnp.dot(bg.T.astype(BF16), jnp.where(in_grp, xdtd_b, zero_b),
                                            preferred_element_type=F32)
            y_off = y_off + jnp.dot(cg_b, jnp.where(in_grp, prev_b, jnp.zeros_like(prev_b)),
                                    preferred_element_type=F32)
        state_ref[...] = new_state
        y = y + y_off * jnp.exp(acs_e)
        y = y * _silu(z_ref[rs, :].astype(F32))
        ysq = y * y
        in0 = lane_in < grp_w
        ms0 = jnp.sum(jnp.where(in0, ysq, 0.0), axis=1, keepdims=True) * (1.0 / grp_w)
        ms1 = jnp.sum(jnp.where(in0, 0.0, ysq), axis=1, keepdims=True) * (1.0 / grp_w)
        r = jnp.where(in0, lax.rsqrt(ms0 + EPS), lax.rsqrt(ms1 + EPS))
        o_ref[rs, :] = ((y * r) * gain_ref[...]).astype(o_ref.dtype)


def _ssd(z, xbc, dt, cw, cb, dtb, alog, dvec, gain, tri):
    B, S, _ = z.shape
    lc = SSD_LC
    tok = lambda w: pl.BlockSpec((None, lc, w), lambda b, c: (b, c, 0))
    consts = (cw, cb, dtb, alog, dvec, gain, tri)
    return pl.pallas_call(
        _ssd_kernel,
        grid=(B, S // lc),
        in_specs=[tok(SSD_INNER), tok(SSD_XBC), tok(LANES)] + [_const_spec(a.shape) for a in consts],
        out_specs=tok(SSD_INNER),
        out_shape=jax.ShapeDtypeStruct((B, S, SSD_INNER), BF16),
        scratch_shapes=[pltpu.VMEM((SSD_N, SSD_INNER), F32), pltpu.VMEM((SUBLANES, SSD_XBC), F32)],
        compiler_params=_cparams(("arbitrary", "arbitrary")),
        name="ssd",
    )(z, xbc, dt, *consts)


def _rms(x, gain):
    ms = jnp.mean(x * x, axis=-1, keepdims=True)
    return (x * lax.rsqrt(ms + EPS)) * gain


def _out_ffn_kernel(x_ref, od_ref, ol_ref, os_ref, wo_d_ref, wo_l_ref, wo_s_ref, g_mix_ref, g_pre_ref,
                    up_ref, cw_ref, cb_ref, down_ref, g_post_ref, o_ref, tail_ref, *, tiles_per_seq):
    t = pl.program_id(0)
    tm = x_ref.shape[0]

    @pl.when(t % tiles_per_seq == 0)
    def _():
        tail_ref[...] = jnp.zeros(tail_ref.shape, F32)

    mix = (jnp.dot(od_ref[...], wo_d_ref[...], preferred_element_type=F32)
           + jnp.dot(ol_ref[...], wo_l_ref[...], preferred_element_type=F32)
           + jnp.dot(os_ref[...], wo_s_ref[...], preferred_element_type=F32))
    x1 = x_ref[...] + _rms(mix, g_mix_ref[...])
    h = _rms(x1, g_pre_ref[...]).astype(BF16)

    f_acc = jnp.zeros((tm, D_MODEL), F32)
    for ci in range(D_FF // FF_CHUNK):
        halves = []
        for off in (0, D_FF):
            cs = slice(off + ci * FF_CHUNK, off + (ci + 1) * FF_CHUNK)
            u = jnp.dot(h, up_ref[:, cs], preferred_element_type=F32)
            tail8 = tail_ref[:, cs]
            cw = cw_ref[:, cs]
            conv = u * cw[FFN_CONV - 1:FFN_CONV, :] + cb_ref[:, cs]
            for kk in range(FFN_CONV - 1):
                conv = conv + _shift_rows(u, tail8, FFN_CONV - 1 - kk) * cw[kk:kk + 1, :]
            tail_ref[:, cs] = u[tm - SUBLANES:tm, :]
            halves.append(conv)
        f = (_silu(halves[0]) * halves[1]).astype(BF16)
        f_acc = f_acc + jnp.dot(f, down_ref[ci * FF_CHUNK:(ci + 1) * FF_CHUNK, :], preferred_element_type=F32)
    o_ref[...] = x1 + _rms(f_acc, g_post_ref[...])


def _out_ffn(x2d, od, ol, os_, wo_d, wo_l, wo_s, g_mix, g_pre, up, cw, cb, down, g_post, tiles_per_seq):
    T, D = x2d.shape
    tm = TM_FFN
    tok = lambda w: pl.BlockSpec((tm, w), lambda t: (t, 0))
    consts = (wo_d, wo_l, wo_s, g_mix, g_pre, up, cw, cb, down, g_post)
    return pl.pallas_call(
        functools.partial(_out_ffn_kernel, tiles_per_seq=tiles_per_seq),
        grid=(T // tm,),
        in_specs=[tok(D), tok(DIFF_W), tok(DIL_W), tok(SSD_INNER)]
                 + [pl.BlockSpec(a.shape, lambda t: (0, 0), pipeline_mode=pl.Buffered(1)) for a in consts],
        out_specs=tok(D),
        out_shape=jax.ShapeDtypeStruct((T, D), F32),
        scratch_shapes=[pltpu.VMEM((SUBLANES, 2 * D_FF), F32)],
        compiler_params=_cparams(("arbitrary",)),
        name="out_ffn",
    )(x2d, od, ol, os_, *consts)


def _rope_tables(S, half):
    inv_freq = jnp.exp(-math.log(ROPE_THETA) * jnp.arange(half, dtype=F32) / half)
    ang = jnp.arange(S, dtype=F32)[:, None] * inv_freq[None, :]
    cos, sin = jnp.cos(ang), jnp.sin(ang)
    zero = jnp.zeros_like(sin)
    reps = LANES // (2 * half)
    cos_t = jnp.tile(jnp.concatenate([cos, cos], axis=1), (1, reps))
    sin_up = jnp.tile(jnp.concatenate([-sin, zero], axis=1), (1, reps))
    sin_dn = jnp.tile(jnp.concatenate([zero, sin], axis=1), (1, reps))
    return cos_t, sin_up, sin_dn


def kernel(x, pre_mix_norm, w_in, diff_lambda, diff_head_norm, ssd_conv_w, ssd_conv_b, ssd_dt_bias, ssd_A_log,
           ssd_D, ssd_norm, w_out, post_mix_norm, pre_ffn_norm, ffn_up, ffn_conv_w, ffn_conv_b, ffn_down,
           post_ffn_norm):
    B, S, D = x.shape
    depth = w_in.shape[0]
    assert D == D_MODEL and S % DIL_CH == 0 and S % TM_IN == 0 and S % BQ == 0 and S % TM_FFN == 0
    tables = _rope_tables(S, DIFF_QK // 2) + _rope_tables(S, DIL_DH // 2)
    gmat = (lax.broadcasted_iota(jnp.int32, (DIFF_W, DIFF_W), 0) // DIFF_V
            == lax.broadcasted_iota(jnp.int32, (DIFF_W, DIFF_W), 1) // DIFF_V).astype(BF16) * (1.0 / DIFF_V)
    gmat = gmat.astype(BF16)
    tri = (lax.broadcasted_iota(jnp.int32, (SSD_CHUNK, SSD_CHUNK), 1)
           <= lax.broadcasted_iota(jnp.int32, (SSD_CHUNK, SSD_CHUNK), 0)).astype(BF16)
    offs = [0]
    for sz in IN_SIZES:
        offs.append(offs[-1] + sz)

    def pad_lanes(v):
        return jnp.pad(v, ((0, 0), (0, LANES - v.shape[1])))

    for layer in range(depth):
        lambda_init = 0.8 - 0.6 * math.exp(-0.3 * layer)
        wl = w_in[layer].astype(BF16)
        ws = [wl[:, offs[i]:offs[i + 1]] for i in range(len(IN_SIZES))]
        ws[-1] = pad_lanes(ws[-1])
        dq, dk, dv, lq, lk, lv, z, xbc, dt = _in_proj(x, pre_mix_norm[layer][None, :], tables, ws)

        o_diff = _diff_attn(dq, dk, dv, diff_lambda[layer],
                            jnp.tile(diff_head_norm[layer], DIFF_HEADS)[None, :], gmat, lambda_init)
        o_dil = _dil_attn(lq, lk, lv)
        o_ssd = _ssd(z, xbc, dt, ssd_conv_w[layer], ssd_conv_b[layer][None, :],
                     pad_lanes(ssd_dt_bias[layer][None, :]), pad_lanes(ssd_A_log[layer][None, :]),
                     jnp.repeat(ssd_D[layer], SSD_P)[None, :], ssd_norm[layer][None, :], tri)

        wo = w_out[layer].astype(BF16)
        x2d = _out_ffn(
            x.reshape(B * S, D), o_diff.reshape(B * S, DIFF_W), o_dil.reshape(B * S, DIL_W),
            o_ssd.reshape(B * S, SSD_INNER),
            wo[0:DIFF_W], wo[DIFF_W:DIFF_W + DIL_W], wo[DIFF_W + DIL_W:],
            post_mix_norm[layer][None, :], pre_ffn_norm[layer][None, :],
            ffn_up[layer].astype(BF16), ffn_conv_w[layer], ffn_conv_b[layer][None, :],
            ffn_down[layer].astype(BF16), post_ffn_norm[layer][None, :], S // TM_FFN)
        x = x2d.reshape(B, S, D)
    return x
```

```python
import functools
import math

import jax
import jax.numpy as jnp
from jax import lax
from jax.experimental import pallas as pl
from jax.experimental.pallas import tpu as pltpu

F32 = jnp.float32
BF16 = jnp.bfloat16

LANES = 128
SUBLANES = 8
VMEM_LIMIT = 56 * 1024 * 1024

D_MODEL = 1024
DIFF_HEADS = 4
DIFF_QK = 32
DIFF_V = 64
DIFF_W = DIFF_HEADS * DIFF_V
DIL_HEADS = 6
DIL_DH = 64
DIL_W = DIL_HEADS * DIL_DH
DIL_PATTERNS = ((128, 1), (512, 4), (2048, 16))
DIL_BLOCK = 128
SSD_HEADS = 6
SSD_P = 64
SSD_GROUPS = 2
SSD_N = 128
SSD_CONV = 4
SSD_CHUNK = 128
SSD_INNER = SSD_HEADS * SSD_P
SSD_XBC = SSD_INNER + 2 * SSD_GROUPS * SSD_N
D_FF = 2816
FFN_CONV = 3
ROPE_THETA = 10000.0
EPS = 1e-6
LOG2E = 1.4426950408889634

IN_SIZES = (DIFF_W, DIFF_W, DIFF_W, DIL_W, DIL_W, DIL_W, SSD_INNER, SSD_XBC, SSD_HEADS)

TM_IN = 512
BQ = 256
DIL_CH = 2048
SSD_LC = 512
TM_FFN = 256
FF_CHUNK = 256


def _cparams(sem):
    return pltpu.CompilerParams(dimension_semantics=sem, vmem_limit_bytes=VMEM_LIMIT)


def _const_spec(shape):
    nd = len(shape)
    return pl.BlockSpec(shape, lambda *_: (0,) * nd)


def _silu(x):
    return x * (1.0 / (1.0 + jnp.exp(-x)))


def _split3(x):
    hi = x.astype(BF16)
    r = x - hi.astype(F32)
    mid = r.astype(BF16)
    lo = (r - mid.astype(F32)).astype(BF16)
    return hi, mid, lo


def _rope_lanes(y, cos, sin_up, sin_dn, half):
    outs = []
    for j in range(y.shape[1] // LANES):
        yc = y[:, j * LANES:(j + 1) * LANES]
        up = pltpu.roll(yc, LANES - half, 1)
        dn = pltpu.roll(yc, half, 1)
        outs.append(yc * cos + up * sin_up + dn * sin_dn)
    return jnp.concatenate(outs, axis=1)


def _rope_rows(yt, cos_t, sin_t, half):
    outs = []
    for g in range(yt.shape[0] // (2 * half)):
        x1 = yt[2 * half * g:2 * half * g + half, :]
        x2 = yt[2 * half * g + half:2 * half * (g + 1), :]
        outs += [x1 * cos_t - x2 * sin_t, x2 * cos_t + x1 * sin_t]
    return jnp.concatenate(outs, axis=0)


def _in_proj_kernel(x_ref, g_ref, cdt_ref, sdt_ref, cd_ref, sud_ref, sdd_ref, cl_ref, sul_ref, sdl_ref,
                    wqt_ref, wk_ref, wvt_ref, wlq_ref, wlk_ref, wlv_ref, wz_ref, wxbc_ref, wdt_ref,
                    oqt_ref, ok_ref, ovt_ref, olq_ref, olk_ref, olv_ref, oz_ref, oxbc_ref, odt_ref):
    x = x_ref[...]
    ms = jnp.mean(x * x, axis=-1, keepdims=True)
    h = ((x * lax.rsqrt(ms + EPS)) * g_ref[...]).astype(BF16)

    def proj(w_ref):
        return jnp.dot(h, w_ref[...], preferred_element_type=F32)

    def proj_t(wt_ref):
        return lax.dot_general(wt_ref[...], h, (((1,), (1,)), ((), ())), preferred_element_type=F32)

    qt = _rope_rows(proj_t(wqt_ref), cdt_ref[...], sdt_ref[...], DIFF_QK // 2)
    oqt_ref[...] = (qt * (DIFF_QK ** -0.5 * LOG2E)).astype(BF16)
    cd, sud, sdd = cd_ref[...], sud_ref[...], sdd_ref[...]
    ok_ref[...] = _rope_lanes(proj(wk_ref), cd, sud, sdd, DIFF_QK // 2).astype(BF16)
    ovt_ref[...] = proj_t(wvt_ref).astype(BF16)
    cl, sul, sdl = cl_ref[...], sul_ref[...], sdl_ref[...]
    lq = _rope_lanes(proj(wlq_ref), cl, sul, sdl, DIL_DH // 2)
    olq_ref[...] = (lq * (DIL_DH ** -0.5)).astype(BF16)
    olk_ref[...] = _rope_lanes(proj(wlk_ref), cl, sul, sdl, DIL_DH // 2).astype(BF16)
    olv_ref[...] = proj(wlv_ref).astype(BF16)
    oz_ref[...] = proj(wz_ref).astype(BF16)
    oxbc_ref[...] = proj(wxbc_ref)
    odt_ref[...] = proj(wdt_ref)


def _in_proj(x, gain, tables, ws):
    B, S, D = x.shape
    tm = TM_IN
    grid = (S // tm, B)
    tok = lambda w: pl.BlockSpec((None, tm, w), lambda s, b: (b, s, 0))
    tok_t = lambda w: pl.BlockSpec((None, w, tm), lambda s, b: (b, 0, s))
    tab = pl.BlockSpec((tm, LANES), lambda s, b: (s, 0))
    tab_t = pl.BlockSpec((DIFF_QK // 2, tm), lambda s, b: (0, s))
    out_widths = (DIFF_W, DIFF_W, DIFF_W, DIL_W, DIL_W, DIL_W, SSD_INNER, SSD_XBC, LANES)
    out_dtypes = (BF16, BF16, BF16, BF16, BF16, BF16, BF16, F32, F32)
    transposed = (True, False, True) + (False,) * 6
    return pl.pallas_call(
        _in_proj_kernel,
        grid=grid,
        in_specs=[tok(D), _const_spec((1, D))] + [tab_t] * 2 + [tab] * 6 + [_const_spec(w.shape) for w in ws],
        out_specs=[tok_t(w) if t else tok(w) for w, t in zip(out_widths, transposed)],
        out_shape=[jax.ShapeDtypeStruct((B, w, S) if t else (B, S, w), dt)
                   for w, dt, t in zip(out_widths, out_dtypes, transposed)],
        compiler_params=_cparams(("arbitrary", "arbitrary")),
        name="in_proj",
    )(x, gain, *tables, *ws)


N_SM = 2 * DIFF_HEADS
ACC_ROWS = DIFF_V + 16


def _diff_attn_kernel(lam_ref, gain_ref, qt_ref, k_ref, vt_ref, o_ref, qs_ref, m_ref, acc_ref, *, lambda_init):
    i = pl.program_id(1)
    bq = qt_ref.shape[1]
    qt = qt_ref[...]
    feat = lax.broadcasted_iota(jnp.int32, (DIFF_W, bq), 0)
    zero = jnp.zeros_like(qt)
    for g in range(N_SM):
        qs_ref[:, g * bq:(g + 1) * bq] = jnp.where(feat // DIFF_QK == g, qt, zero)
    m_ref[...] = jnp.full(m_ref.shape, -jnp.inf, F32)
    acc_ref[...] = jnp.zeros(acc_ref.shape, F32)
    ones = jnp.ones((ACC_ROWS - DIFF_V, bq), BF16)

    def step(j, masked):
        start = pl.multiple_of(j * bq, bq)
        kj = k_ref[pl.ds(start, bq), :]
        vtj = vt_ref[:, pl.ds(start, bq)]
        if masked:
            key = lax.broadcasted_iota(jnp.int32, (bq, bq), 0)
            qry = lax.broadcasted_iota(jnp.int32, (bq, bq), 1)
            visible = key <= qry
        m_all = m_ref[...]
        new_m, new_acc = [], []

        def scores(g):
            return jnp.dot(kj, qs_ref[:, g * bq:(g + 1) * bq], preferred_element_type=F32)

        ahead = 8
        s_tiles = [scores(g) for g in range(ahead)] + [None] * (N_SM - ahead)
        for g in range(N_SM):
            s = s_tiles[g]
            if masked:
                s = jnp.where(visible, s, -jnp.inf)
            m_prev = m_all[:, g * bq:(g + 1) * bq]
            m_new = jnp.maximum(m_prev, jnp.max(s, axis=0, keepdims=True))
            alpha = jnp.exp2(m_prev - m_new)
            p = jnp.exp2(s - m_new).astype(BF16)
            if g + ahead < N_SM:
                s_tiles[g + ahead] = scores(g + ahead)
            hd = g // 2
            lhs = jnp.concatenate([vtj[hd * DIFF_V:(hd + 1) * DIFF_V, :], ones], axis=0)
            pv = jnp.dot(lhs, p, preferred_element_type=F32)
            new_m.append(m_new)
            new_acc.append(acc_ref[g] * alpha + pv)
        m_ref[...] = jnp.concatenate(new_m, axis=1)
        for g in range(N_SM):
            acc_ref[g] = new_acc[g]

    def body(j, carry):
        step(j, False)
        return carry

    lax.fori_loop(0, i, body, 0)
    step(i, True)

    lam_p = lam_ref[...]
    s1 = jnp.sum(lam_p[0:1, :] * lam_p[1:2, :], axis=1, keepdims=True)
    s2 = jnp.sum(lam_p[2:3, :] * lam_p[3:4, :], axis=1, keepdims=True)
    lam = jnp.exp(s1) - jnp.exp(s2) + lambda_init
    gain = pltpu.repeat(gain_ref[...], bq // LANES, axis=1)
    outs = []
    for hd in range(DIFF_HEADS):
        a1 = acc_ref[2 * hd]
        a2 = acc_ref[2 * hd + 1]
        o1 = a1[0:DIFF_V, :] / a1[DIFF_V:DIFF_V + 1, :]
        o2 = a2[0:DIFF_V, :] / a2[DIFF_V:DIFF_V + 1, :]
        d = o1 - lam * o2
        ms = jnp.mean(d * d, axis=0, keepdims=True)
        outs.append(((d * lax.rsqrt(ms + EPS)) * gain) * (1.0 - lambda_init))
    o_ref[...] = jnp.concatenate(outs, axis=0).T.astype(o_ref.dtype)


def _diff_attn(qt, k, vt, lam_p, gain_b, lambda_init):
    B, S, W = k.shape
    grid = (B, S // BQ)
    return pl.pallas_call(
        functools.partial(_diff_attn_kernel, lambda_init=lambda_init),
        grid=grid,
        in_specs=[_const_spec(lam_p.shape), _const_spec(gain_b.shape),
                  pl.BlockSpec((None, W, BQ), lambda b, i: (b, 0, i)),
                  pl.BlockSpec((None, S, W), lambda b, i: (b, 0, 0)),
                  pl.BlockSpec((None, W, S), lambda b, i: (b, 0, 0))],
        out_specs=pl.BlockSpec((None, BQ, W), lambda b, i: (b, i, 0)),
        out_shape=jax.ShapeDtypeStruct((B, S, W), BF16),
        scratch_shapes=[pltpu.VMEM((W, N_SM * BQ), BF16),
                        pltpu.VMEM((1, N_SM * BQ), F32),
                        pltpu.VMEM((N_SM, ACC_ROWS, BQ), F32)],
        compiler_params=_cparams(("arbitrary", "arbitrary")),
        name="diff_attn",
    )(lam_p, gain_b, qt, k, vt)


N_PAIRS = DIL_W // LANES


def _dil_pair(qp, kp, vp, first_block):
    nq, nk = DIL_BLOCK, 2 * DIL_BLOCK
    iq = lax.broadcasted_iota(jnp.int32, (nq, nk), 0)
    jk = lax.broadcasted_iota(jnp.int32, (nq, nk), 1)
    band = (jk >= iq) & (jk <= iq + DIL_BLOCK)
    valid = band & (jk >= jnp.where(first_block, DIL_BLOCK, 0))
    low = lax.broadcasted_iota(jnp.int32, (nq, LANES), 1) < DIL_DH
    kb = kp.astype(BF16)
    vb = vp.astype(BF16)
    res = []
    for e in range(2):
        keep = low if e == 0 else jnp.logical_not(low)
        qm = jnp.where(keep, qp, 0.0).astype(BF16)
        s = lax.dot_general(qm, kb, (((1,), (1,)), ((), ())), preferred_element_type=F32)
        s = jnp.where(valid, s, -jnp.inf)
        m = jnp.max(s, axis=1, keepdims=True)
        p = jnp.exp(s - m)
        den = jnp.sum(p, axis=1, keepdims=True)
        num = jnp.dot(p.astype(BF16), vb, preferred_element_type=F32)
        res.append((num, den, m))
    return tuple(jnp.where(low, res[0][t], res[1][t]) for t in range(3))


def _dil_merge(a, b):
    (n1, d1, m1), (n2, d2, m2) = a, b
    m = jnp.maximum(m1, m2)
    w1 = jnp.exp(m1 - m)
    w2 = jnp.exp(m2 - m)
    return n1 * w1 + n2 * w2, d1 * w1 + d2 * w2, m


def _dil_attn_kernel(q_ref, k_ref, v_ref, o_ref, qf_ref, kk_ref, vv_ref, num_ref, den_ref, max_ref):
    c = pl.program_id(1)
    ch = q_ref.shape[0]

    @pl.when(c == 0)
    def _():
        kk_ref[:, 0:ch, :] = jnp.zeros((N_PAIRS, ch, LANES), F32)
        vv_ref[:, 0:ch, :] = jnp.zeros((N_PAIRS, ch, LANES), F32)

    for pr in range(N_PAIRS):
        sl = slice(pr * LANES, (pr + 1) * LANES)
        qf_ref[pr] = q_ref[:, sl].astype(F32)
        kk_ref[pr, ch:2 * ch, :] = k_ref[:, sl].astype(F32)
        vv_ref[pr, ch:2 * ch, :] = v_ref[:, sl].astype(F32)
    first_chunk = c == 0

    def block(pr, rows, keys, first_block):
        return _dil_pair(qf_ref[pr, rows, :], kk_ref[pr, keys, :], vv_ref[pr, keys, :], first_block)

    def merged(pr, rows, new):
        return _dil_merge((num_ref[pr, rows, :], den_ref[pr, rows, :], max_ref[pr, rows, :]), new)

    def store(pr, rows, ndm):
        num_ref[pr, rows, :] = ndm[0]
        den_ref[pr, rows, :] = ndm[1]
        max_ref[pr, rows, :] = ndm[2]

    d16 = DIL_PATTERNS[2][1]

    def body16(r, carry):
        rows = pl.ds(r, DIL_BLOCK, stride=d16)
        keys = pl.ds(r, 2 * DIL_BLOCK, stride=d16)
        for pr in range(N_PAIRS):
            store(pr, rows, block(pr, rows, keys, first_chunk))
        return carry

    lax.fori_loop(0, d16, body16, 0)

    d4 = DIL_PATTERNS[1][1]
    blocks4 = ch // (d4 * DIL_BLOCK)

    def body4(t, carry):
        r = t % d4
        nb = t // d4
        q0 = r + nb * (d4 * DIL_BLOCK)
        rows = pl.ds(q0, DIL_BLOCK, stride=d4)
        keys = pl.ds(ch + q0 - d4 * DIL_BLOCK, 2 * DIL_BLOCK, stride=d4)
        first_block = jnp.logical_and(first_chunk, nb == 0)
        for pr in range(N_PAIRS):
            store(pr, rows, merged(pr, rows, block(pr, rows, keys, first_block)))
        return carry

    lax.fori_loop(0, d4 * blocks4, body4, 0)

    def body1(nb, carry):
        q0 = pl.multiple_of(nb * DIL_BLOCK, DIL_BLOCK)
        rows = pl.ds(q0, DIL_BLOCK)
        keys = pl.ds(pl.multiple_of(ch + q0 - DIL_BLOCK, DIL_BLOCK), 2 * DIL_BLOCK)
        first_block = jnp.logical_and(first_chunk, nb == 0)
        for pr in range(N_PAIRS):
            n, d, _ = merged(pr, rows, block(pr, rows, keys, first_block))
            o_ref[rows, pr * LANES:(pr + 1) * LANES] = (n / d).astype(o_ref.dtype)
        return carry

    lax.fori_loop(0, ch // DIL_BLOCK, body1, 0)

    kk_ref[:, 0:ch, :] = kk_ref[:, ch:2 * ch, :]
    vv_ref[:, 0:ch, :] = vv_ref[:, ch:2 * ch, :]


def _dil_attn(q, k, v):
    B, S, W = q.shape
    ch = DIL_CH
    blk = pl.BlockSpec((None, ch, W), lambda b, c: (b, c, 0))
    return pl.pallas_call(
        _dil_attn_kernel,
        grid=(B, S // ch),
        in_specs=[blk, blk, blk],
        out_specs=blk,
        out_shape=jax.ShapeDtypeStruct((B, S, W), BF16),
        scratch_shapes=[pltpu.VMEM((N_PAIRS, ch, LANES), F32), pltpu.VMEM((N_PAIRS, 2 * ch, LANES), F32),
                        pltpu.VMEM((N_PAIRS, 2 * ch, LANES), F32), pltpu.VMEM((N_PAIRS, ch, LANES), F32),
                        pltpu.VMEM((N_PAIRS, ch, LANES), F32), pltpu.VMEM((N_PAIRS, ch, LANES), F32)],
        compiler_params=_cparams(("arbitrary", "arbitrary")),
        name="dil_attn",
    )(q, k, v)


def _shift_rows(x, tail8, s):
    n = x.shape[0]
    row = lax.broadcasted_iota(jnp.int32, x.shape, 0)
    head = pltpu.repeat(pltpu.roll(tail8, s, 0), n // SUBLANES, axis=0)
    return jnp.where(row < s, head, pltpu.roll(x, s, 0))


def _expand_heads(v, n_heads, width):
    rows = v.shape[0]
    lane = lax.broadcasted_iota(jnp.int32, (rows, n_heads * width), 1)
    out = jnp.zeros((rows, n_heads * width), F32)
    for h in range(n_heads):
        out = jnp.where(lane // width == h, v[:, h:h + 1], out)
    return out


def _ssd_kernel(z_ref, xbc_ref, dt_ref, cw_ref, cb_ref, dtb_ref, alog_ref, dvec_ref, gain_ref, tri_ref,
                o_ref, state_ref, tail_ref):
    c = pl.program_id(1)
    lc = xbc_ref.shape[0]
    L = SSD_CHUNK

    @pl.when(c == 0)
    def _():
        state_ref[...] = jnp.zeros(state_ref.shape, F32)
        tail_ref[...] = jnp.zeros(tail_ref.shape, F32)

    xbc = xbc_ref[...]
    tail8 = tail_ref[...]
    cw = cw_ref[...]
    acc = xbc * cw[SSD_CONV - 1:SSD_CONV, :] + cb_ref[...]
    for kk in range(SSD_CONV - 1):
        acc = acc + _shift_rows(xbc, tail8, SSD_CONV - 1 - kk) * cw[kk:kk + 1, :]
    tail_ref[...] = xbc[lc - SUBLANES:lc, :]
    xbc_c = _silu(acc)

    xs_all = xbc_c[:, 0:SSD_INNER]
    dt_raw = dt_ref[...] + dtb_ref[...]
    dt_all = jnp.maximum(dt_raw, 0.0) + jnp.log(1.0 + jnp.exp(-jnp.abs(dt_raw)))
    a_all = dt_all * (-jnp.exp(alog_ref[...]))
    tri = tri_ref[...]
    lane_in = lax.broadcasted_iota(jnp.int32, (L, SSD_INNER), 1)
    grp_w = SSD_INNER // SSD_GROUPS
    ii = lax.broadcasted_iota(jnp.int32, (L, L), 0)
    jj = lax.broadcasted_iota(jnp.int32, (L, L), 1)
    causal = jj <= ii

    for ck in range(lc // L):
        rs = slice(ck * L, (ck + 1) * L)
        xs = xs_all[rs, :]
        a = a_all[rs, :]
        hi, mid, lo = _split3(a)
        acs = (jnp.dot(tri, hi, preferred_element_type=F32) + jnp.dot(tri, mid, preferred_element_type=F32)
               + jnp.dot(tri, lo, preferred_element_type=F32))
        acs_t = acs.T
        dt_e = _expand_heads(dt_all[rs, :], SSD_HEADS, SSD_P)
        acs_e = _expand_heads(acs, SSD_HEADS, SSD_P)
        last_e = acs_e[L - 1:L, :]
        xdt = xs * dt_e
        xdt_b = xdt.astype(BF16)
        xdtd_b = (xdt * jnp.exp(last_e - acs_e)).astype(BF16)
        prev = state_ref[...]
        prev_b = prev.astype(BF16)
        y = xs * dvec_ref[...]
        new_state = prev * jnp.exp(last_e)
        y_off = jnp.zeros((L, SSD_INNER), F32)
        for g in range(SSD_GROUPS):
            bg = xbc_c[rs, SSD_INNER + g * SSD_N:SSD_INNER + (g + 1) * SSD_N]
            cg = xbc_c[rs, SSD_INNER + (SSD_GROUPS + g) * SSD_N:SSD_INNER + (SSD_GROUPS + g + 1) * SSD_N]
            bg_b = bg.astype(BF16)
            cg_b = cg.astype(BF16)
            in_grp = lane_in // grp_w == g
            cb = lax.dot_general(cg_b, bg_b, (((1,), (1,)), ((), ())), preferred_element_type=F32)
            for hh in range(SSD_HEADS // SSD_GROUPS):
                hd = g * (SSD_HEADS // SSD_GROUPS) + hh
                seg = acs[:, hd:hd + 1] - acs_t[hd:hd + 1, :]
                w = jnp.where(causal, cb * jnp.exp(seg), 0.0).astype(BF16)
                y = y + jnp.dot(w, jnp.where(lane_in // SSD_P == hd, xdt_b, jnp.zeros_like(xdt_b)),
                                preferred_element_type=F32)
            zero_b = jnp.zeros_like(xdtd_b)
            new_state = new_state + jnp.dot(bg.T.astype(BF16), jnp.where(in_grp, xdtd_b, zero_b),
                                            preferred_element_type=F32)
            y_off = y_off + jnp.dot(cg_b, jnp.where(in_grp, prev_b, jnp.zeros_like(prev_b)),
                                    preferred_element_type=F32)
        state_ref[...] = new_state
        y = y + y_off * jnp.exp(acs_e)
        y = y * _silu(z_ref[rs, :].astype(F32))
        ysq = y * y
        in0 = lane_in < grp_w
        ms0 = jnp.sum(jnp.where(in0, ysq, 0.0), axis=1, keepdims=True) * (1.0 / grp_w)
        ms1 = jnp.sum(jnp.where(in0, 0.0, ysq), axis=1, keepdims=True) * (1.0 / grp_w)
        r = jnp.where(in0, lax.rsqrt(ms0 + EPS), lax.rsqrt(ms1 + EPS))
        o_ref[rs, :] = ((y * r) * gain_ref[...]).astype(o_ref.dtype)


def _ssd(z, xbc, dt, cw, cb, dtb, alog, dvec, gain, tri):
    B, S, _ = z.shape
    lc = SSD_LC
    tok = lambda w: pl.BlockSpec((None, lc, w), lambda b, c: (b, c, 0))
    consts = (cw, cb, dtb, alog, dvec, gain, tri)
    return pl.pallas_call(
        _ssd_kernel,
        grid=(B, S // lc),
        in_specs=[tok(SSD_INNER), tok(SSD_XBC), tok(LANES)] + [_const_spec(a.shape) for a in consts],
        out_specs=tok(SSD_INNER),
        out_shape=jax.ShapeDtypeStruct((B, S, SSD_INNER), BF16),
        scratch_shapes=[pltpu.VMEM((SSD_N, SSD_INNER), F32), pltpu.VMEM((SUBLANES, SSD_XBC), F32)],
        compiler_params=_cparams(("arbitrary", "arbitrary")),
        name="ssd",
    )(z, xbc, dt, *consts)


def _rms(x, gain):
    ms = jnp.mean(x * x, axis=-1, keepdims=True)
    return (x * lax.rsqrt(ms + EPS)) * gain


def _out_ffn_kernel(x_ref, od_ref, ol_ref, os_ref, wo_d_ref, wo_l_ref, wo_s_ref, g_mix_ref, g_pre_ref,
                    up_ref, cw_ref, cb_ref, down_ref, g_post_ref, o_ref, tail_ref, *, tiles_per_seq):
    t = pl.program_id(0)
    tm = x_ref.shape[0]

    @pl.when(t % tiles_per_seq == 0)
    def _():
        tail_ref[...] = jnp.zeros(tail_ref.shape, F32)

    mix = (jnp.dot(od_ref[...], wo_d_ref[...], preferred_element_type=F32)
           + jnp.dot(ol_ref[...], wo_l_ref[...], preferred_element_type=F32)
           + jnp.dot(os_ref[...], wo_s_ref[...], preferred_element_type=F32))
    x1 = x_ref[...] + _rms(mix, g_mix_ref[...])
    h = _rms(x1, g_pre_ref[...]).astype(BF16)

    f_acc = jnp.zeros((tm, D_MODEL), F32)
    for ci in range(D_FF // FF_CHUNK):
        halves = []
        for off in (0, D_FF):
            cs = slice(off + ci * FF_CHUNK, off + (ci + 1) * FF_CHUNK)
            u = jnp.dot(h, up_ref[:, cs], preferred_element_type=F32)
            tail8 = tail_ref[:, cs]
            cw = cw_ref[:, cs]
            conv = u * cw[FFN_CONV - 1:FFN_CONV, :] + cb_ref[:, cs]
            for kk in range(FFN_CONV - 1):
                conv = conv + _shift_rows(u, tail8, FFN_CONV - 1 - kk) * cw[kk:kk + 1, :]
            tail_ref[:, cs] = u[tm - SUBLANES:tm, :]
            halves.append(conv)
        f = (_silu(halves[0]) * halves[1]).astype(BF16)
        f_acc = f_acc + jnp.dot(f, down_ref[ci * FF_CHUNK:(ci + 1) * FF_CHUNK, :], preferred_element_type=F32)
    o_ref[...] = x1 + _rms(f_acc, g_post_ref[...])


def _out_ffn(x2d, od, ol, os_, wo_d, wo_l, wo_s, g_mix, g_pre, up, cw, cb, down, g_post, tiles_per_seq):
    T, D = x2d.shape
    tm = TM_FFN
    tok = lambda w: pl.BlockSpec((tm, w), lambda t: (t, 0))
    consts = (wo_d, wo_l, wo_s, g_mix, g_pre, up, cw, cb, down, g_post)
    return pl.pallas_call(
        functools.partial(_out_ffn_kernel, tiles_per_seq=tiles_per_seq),
        grid=(T // tm,),
        in_specs=[tok(D), tok(DIFF_W), tok(DIL_W), tok(SSD_INNER)]
                 + [pl.BlockSpec(a.shape, lambda t: (0, 0), pipeline_mode=pl.Buffered(1)) for a in consts],
        out_specs=tok(D),
        out_shape=jax.ShapeDtypeStruct((T, D), F32),
        scratch_shapes=[pltpu.VMEM((SUBLANES, 2 * D_FF), F32)],
        compiler_params=_cparams(("arbitrary",)),
        name="out_ffn",
    )(x2d, od, ol, os_, *consts)


def _rope_tables(S, half):
    inv_freq = jnp.exp(-math.log(ROPE_THETA) * jnp.arange(half, dtype=F32) / half)
    ang = jnp.arange(S, dtype=F32)[:, None] * inv_freq[None, :]
    cos, sin = jnp.cos(ang), jnp.sin(ang)
    zero = jnp.zeros_like(sin)
    reps = LANES // (2 * half)
    cos_t = jnp.tile(jnp.concatenate([cos, cos], axis=1), (1, reps))
    sin_up = jnp.tile(jnp.concatenate([-sin, zero], axis=1), (1, reps))
    sin_dn = jnp.tile(jnp.concatenate([zero, sin], axis=1), (1, reps))
    return cos_t, sin_up, sin_dn


def _rope_tables_t(S, half):
    inv_freq = jnp.exp(-math.log(ROPE_THETA) * jnp.arange(half, dtype=F32) / half)
    ang = jnp.arange(S, dtype=F32)[:, None] * inv_freq[None, :]
    return jnp.cos(ang).T, jnp.sin(ang).T


def kernel(x, pre_mix_norm, w_in, diff_lambda, diff_head_norm, ssd_conv_w, ssd_conv_b, ssd_dt_bias, ssd_A_log,
           ssd_D, ssd_norm, w_out, post_mix_norm, pre_ffn_norm, ffn_up, ffn_conv_w, ffn_conv_b, ffn_down,
           post_ffn_norm):
    B, S, D = x.shape
    depth = w_in.shape[0]
    assert D == D_MODEL and S % DIL_CH == 0 and S % TM_IN == 0 and S % BQ == 0 and S % TM_FFN == 0
    tables = _rope_tables_t(S, DIFF_QK // 2) + _rope_tables(S, DIFF_QK // 2) + _rope_tables(S, DIL_DH // 2)
    tri = (lax.broadcasted_iota(jnp.int32, (SSD_CHUNK, SSD_CHUNK), 1)
           <= lax.broadcasted_iota(jnp.int32, (SSD_CHUNK, SSD_CHUNK), 0)).astype(BF16)
    offs = [0]
    for sz in IN_SIZES:
        offs.append(offs[-1] + sz)

    def pad_lanes(v):
        return jnp.pad(v, ((0, 0), (0, LANES - v.shape[1])))

    for layer in range(depth):
        lambda_init = 0.8 - 0.6 * math.exp(-0.3 * layer)
        wl = w_in[layer].astype(BF16)
        ws = [wl[:, offs[i]:offs[i + 1]] for i in range(len(IN_SIZES))]
        ws[-1] = pad_lanes(ws[-1])
        ws[0] = ws[0].T
        ws[2] = ws[2].T
        dqt, dk, dvt, lq, lk, lv, z, xbc, dt = _in_proj(x, pre_mix_norm[layer][None, :], tables, ws)

        o_diff = _diff_attn(dqt, dk, dvt, diff_lambda[layer],
                            jnp.broadcast_to(diff_head_norm[layer][:, None], (DIFF_V, LANES)), lambda_init)
        o_dil = _dil_attn(lq, lk, lv)
        o_ssd = _ssd(z, xbc, dt, ssd_conv_w[layer], ssd_conv_b[layer][None, :],
                     pad_lanes(ssd_dt_bias[layer][None, :]), pad_lanes(ssd_A_log[layer][None, :]),
                     jnp.repeat(ssd_D[layer], SSD_P)[None, :], ssd_norm[layer][None, :], tri)

        wo = w_out[layer].astype(BF16)
        x2d = _out_ffn(
            x.reshape(B * S, D), o_diff.reshape(B * S, DIFF_W), o_dil.reshape(B * S, DIL_W),
            o_ssd.reshape(B * S, SSD_INNER),
            wo[0:DIFF_W], wo[DIFF_W:DIFF_W + DIL_W], wo[DIFF_W + DIL_W:],
            post_mix_norm[layer][None, :], pre_ffn_norm[layer][None, :],
            ffn_up[layer].astype(BF16), ffn_conv_w[layer], ffn_conv_b[layer][None, :],
            ffn_down[layer].astype(BF16), post_ffn_norm[layer][None, :], S // TM_FFN)
        x = x2d.reshape(B, S, D)
    return x
```

```python
import functools
import math

import jax
import jax.numpy as jnp
from jax import lax
from jax.experimental import pallas as pl
from jax.experimental.pallas import tpu as pltpu

F32 = jnp.float32
BF16 = jnp.bfloat16

LANES = 128
SUBLANES = 8
VMEM_LIMIT = 56 * 1024 * 1024

D_MODEL = 1024
DIFF_HEADS = 4
DIFF_QK = 32
DIFF_V = 64
DIFF_W = DIFF_HEADS * DIFF_V
DIL_HEADS = 6
DIL_DH = 64
DIL_W = DIL_HEADS * DIL_DH
DIL_PATTERNS = ((128, 1), (512, 4), (2048, 16))
DIL_BLOCK = 128
SSD_HEADS = 6
SSD_P = 64
SSD_GROUPS = 2
SSD_N = 128
SSD_CONV = 4
SSD_CHUNK = 128
SSD_INNER = SSD_HEADS * SSD_P
SSD_XBC = SSD_INNER + 2 * SSD_GROUPS * SSD_N
D_FF = 2816
FFN_CONV = 3
ROPE_THETA = 10000.0
EPS = 1e-6
LOG2E = 1.4426950408889634

IN_SIZES = (DIFF_W, DIFF_W, DIFF_W, DIL_W, DIL_W, DIL_W, SSD_INNER, SSD_XBC, SSD_HEADS)

TM_IN = 512
BQ = 256
DIL_CH = 2048
SSD_LC = 512
TM_FFN = 256
FF_CHUNK = 256


def _cparams(sem):
    return pltpu.CompilerParams(dimension_semantics=sem, vmem_limit_bytes=VMEM_LIMIT)


def _const_spec(shape):
    nd = len(shape)
    return pl.BlockSpec(shape, lambda *_: (0,) * nd)


def _silu(x):
    return x * (1.0 / (1.0 + jnp.exp(-x)))


def _split3(x):
    hi = x.astype(BF16)
    r = x - hi.astype(F32)
    mid = r.astype(BF16)
    lo = (r - mid.astype(F32)).astype(BF16)
    return hi, mid, lo


def _rope_lanes(y, cos, sin_up, sin_dn, half):
    outs = []
    for j in range(y.shape[1] // LANES):
        yc = y[:, j * LANES:(j + 1) * LANES]
        up = pltpu.roll(yc, LANES - half, 1)
        dn = pltpu.roll(yc, half, 1)
        outs.append(yc * cos + up * sin_up + dn * sin_dn)
    return jnp.concatenate(outs, axis=1)


def _rope_rows(yt, cos_t, sin_t, half):
    outs = []
    for g in range(yt.shape[0] // (2 * half)):
        x1 = yt[2 * half * g:2 * half * g + half, :]
        x2 = yt[2 * half * g + half:2 * half * (g + 1), :]
        outs += [x1 * cos_t - x2 * sin_t, x2 * cos_t + x1 * sin_t]
    return jnp.concatenate(outs, axis=0)


def _in_proj_kernel(x_ref, g_ref, cdt_ref, sdt_ref, cd_ref, sud_ref, sdd_ref, cl_ref, sul_ref, sdl_ref,
                    wqt_ref, wa_ref, wb_ref, wvt_ref,
                    oqt_ref, ok_ref, ovt_ref, olq_ref, olk_ref, olv_ref, oz_ref, oxbc_ref, odt_ref):
    x = x_ref[...]
    ms = jnp.mean(x * x, axis=-1, keepdims=True)
    h = ((x * lax.rsqrt(ms + EPS)) * g_ref[...]).astype(BF16)

    def proj(w_ref):
        return jnp.dot(h, w_ref[...], preferred_element_type=F32)

    def proj_t(wt_ref):
        return lax.dot_general(wt_ref[...], h, (((1,), (1,)), ((), ())), preferred_element_type=F32)

    r_qt = proj_t(wqt_ref)
    r_a = proj(wa_ref)
    r_b = proj(wb_ref)
    r_vt = proj_t(wvt_ref)

    qt = _rope_rows(r_qt, cdt_ref[...], sdt_ref[...], DIFF_QK // 2)
    oqt_ref[...] = (qt * (DIFF_QK ** -0.5 * LOG2E)).astype(BF16)
    cd, sud, sdd = cd_ref[...], sud_ref[...], sdd_ref[...]
    a0, a1, a2 = DIFF_W, DIFF_W + DIL_W, DIFF_W + 2 * DIL_W
    ok_ref[...] = _rope_lanes(r_a[:, 0:a0], cd, sud, sdd, DIFF_QK // 2).astype(BF16)
    cl, sul, sdl = cl_ref[...], sul_ref[...], sdl_ref[...]
    lq = _rope_lanes(r_a[:, a0:a1], cl, sul, sdl, DIL_DH // 2)
    olq_ref[...] = (lq * (DIL_DH ** -0.5)).astype(BF16)
    olk_ref[...] = _rope_lanes(r_a[:, a1:a2], cl, sul, sdl, DIL_DH // 2).astype(BF16)
    b0, b1, b2 = DIL_W, DIL_W + SSD_INNER, DIL_W + SSD_INNER + SSD_XBC
    olv_ref[...] = r_b[:, 0:b0].astype(BF16)
    oz_ref[...] = r_b[:, b0:b1].astype(BF16)
    oxbc_ref[...] = r_b[:, b1:b2]
    odt_ref[...] = r_b[:, b2:b2 + LANES]
    ovt_ref[...] = r_vt.astype(BF16)


def _in_proj(x, gain, tables, ws):
    B, S, D = x.shape
    tm = TM_IN
    grid = (S // tm, B)
    tok = lambda w: pl.BlockSpec((None, tm, w), lambda s, b: (b, s, 0))
    tok_t = lambda w: pl.BlockSpec((None, w, tm), lambda s, b: (b, 0, s))
    tab = pl.BlockSpec((tm, LANES), lambda s, b: (s, 0))
    tab_t = pl.BlockSpec((DIFF_QK // 2, tm), lambda s, b: (0, s))
    out_widths = (DIFF_W, DIFF_W, DIFF_W, DIL_W, DIL_W, DIL_W, SSD_INNER, SSD_XBC, LANES)
    out_dtypes = (BF16, BF16, BF16, BF16, BF16, BF16, BF16, F32, F32)
    transposed = (True, False, True) + (False,) * 6
    return pl.pallas_call(
        _in_proj_kernel,
        grid=grid,
        in_specs=[tok(D), _const_spec((1, D))] + [tab_t] * 2 + [tab] * 6 + [_const_spec(w.shape) for w in ws],
        out_specs=[tok_t(w) if t else tok(w) for w, t in zip(out_widths, transposed)],
        out_shape=[jax.ShapeDtypeStruct((B, w, S) if t else (B, S, w), dt)
                   for w, dt, t in zip(out_widths, out_dtypes, transposed)],
        compiler_params=_cparams(("arbitrary", "arbitrary")),
        name="in_proj",
    )(x, gain, *tables, *ws)


N_SM = 2 * DIFF_HEADS
ACC_ROWS = DIFF_V + 16
AHEAD = 2


def _diff_attn_kernel(lam_ref, gain_ref, qt_ref, k_ref, vt_ref, o_ref, qs_ref, m_ref, acc_ref, sa_ref, sb_ref, *,
                      lambda_init):
    i = pl.program_id(1)
    bq = qt_ref.shape[1]
    qt = qt_ref[...]
    feat = lax.broadcasted_iota(jnp.int32, (DIFF_W, bq), 0)
    zero = jnp.zeros_like(qt)
    for g in range(N_SM):
        qs_ref[:, g * bq:(g + 1) * bq] = jnp.where(feat // DIFF_QK == g, qt, zero)
    m_ref[...] = jnp.full(m_ref.shape, -jnp.inf, F32)
    acc_ref[...] = jnp.zeros(acc_ref.shape, F32)
    ones = jnp.ones((ACC_ROWS - DIFF_V, bq), BF16)

    def key_tile(j):
        return k_ref[pl.ds(pl.multiple_of(j * bq, bq), bq), :]

    def scores_into(dst_ref, kj, g):
        cols = slice(g * bq, (g + 1) * bq)
        dst_ref[:, cols] = jnp.dot(kj, qs_ref[:, cols], preferred_element_type=F32)

    def step(src_ref, j, masked, dst_ref):
        vtj = vt_ref[:, pl.ds(pl.multiple_of(j * bq, bq), bq)]
        if masked:
            key = lax.broadcasted_iota(jnp.int32, (bq, bq), 0)
            qry = lax.broadcasted_iota(jnp.int32, (bq, bq), 1)
            visible = key <= qry
        if dst_ref is not None:
            kn = key_tile(j + 1)
            for g in range(AHEAD):
                scores_into(dst_ref, kn, g)
        m_all = m_ref[...]
        new_m, new_acc = [], []
        for g in range(N_SM):
            s = src_ref[:, g * bq:(g + 1) * bq]
            if masked:
                s = jnp.where(visible, s, -jnp.inf)
            m_prev = m_all[:, g * bq:(g + 1) * bq]
            m_new = jnp.maximum(m_prev, jnp.max(s, axis=0, keepdims=True))
            alpha = jnp.exp2(m_prev - m_new)
            p = jnp.exp2(s - m_new).astype(BF16)
            if dst_ref is not None and g + AHEAD < N_SM:
                scores_into(dst_ref, kn, g + AHEAD)
            hd = g // 2
            lhs = jnp.concatenate([vtj[hd * DIFF_V:(hd + 1) * DIFF_V, :], ones], axis=0)
            pv = jnp.dot(lhs, p, preferred_element_type=F32)
            new_m.append(m_new)
            new_acc.append(acc_ref[g] * alpha + pv)
        m_ref[...] = jnp.concatenate(new_m, axis=1)
        for g in range(N_SM):
            acc_ref[g] = new_acc[g]

    k0 = key_tile(0)
    for g in range(N_SM):
        scores_into(sa_ref, k0, g)

    def body(jj, carry):
        step(sa_ref, 2 * jj, False, sb_ref)
        step(sb_ref, 2 * jj + 1, False, sa_ref)
        return carry

    lax.fori_loop(0, i // 2, body, 0)

    @pl.when(i % 2 == 0)
    def _():
        step(sa_ref, i, True, None)

    @pl.when(i % 2 == 1)
    def _():
        step(sa_ref, i - 1, False, sb_ref)
        step(sb_ref, i, True, None)

    lam_p = lam_ref[...]
    s1 = jnp.sum(lam_p[0:1, :] * lam_p[1:2, :], axis=1, keepdims=True)
    s2 = jnp.sum(lam_p[2:3, :] * lam_p[3:4, :], axis=1, keepdims=True)
    lam = jnp.exp(s1) - jnp.exp(s2) + lambda_init
    gain = pltpu.repeat(gain_ref[...], bq // LANES, axis=1)
    outs = []
    for hd in range(DIFF_HEADS):
        a1 = acc_ref[2 * hd]
        a2 = acc_ref[2 * hd + 1]
        o1 = a1[0:DIFF_V, :] / a1[DIFF_V:DIFF_V + 1, :]
        o2 = a2[0:DIFF_V, :] / a2[DIFF_V:DIFF_V + 1, :]
        d = o1 - lam * o2
        ms = jnp.mean(d * d, axis=0, keepdims=True)
        outs.append(((d * lax.rsqrt(ms + EPS)) * gain) * (1.0 - lambda_init))
    o_ref[...] = jnp.concatenate(outs, axis=0).T.astype(o_ref.dtype)


def _diff_attn(qt, k, vt, lam_p, gain_b, lambda_init):
    B, S, W = k.shape
    grid = (B, S // BQ)
    return pl.pallas_call(
        functools.partial(_diff_attn_kernel, lambda_init=lambda_init),
        grid=grid,
        in_specs=[_const_spec(lam_p.shape), _const_spec(gain_b.shape),
                  pl.BlockSpec((None, W, BQ), lambda b, i: (b, 0, i)),
                  pl.BlockSpec((None, S, W), lambda b, i: (b, 0, 0)),
                  pl.BlockSpec((None, W, S), lambda b, i: (b, 0, 0))],
        out_specs=pl.BlockSpec((None, BQ, W), lambda b, i: (b, i, 0)),
        out_shape=jax.ShapeDtypeStruct((B, S, W), BF16),
        scratch_shapes=[pltpu.VMEM((W, N_SM * BQ), BF16),
                        pltpu.VMEM((1, N_SM * BQ), F32),
                        pltpu.VMEM((N_SM, ACC_ROWS, BQ), F32),
                        pltpu.VMEM((BQ, N_SM * BQ), F32),
                        pltpu.VMEM((BQ, N_SM * BQ), F32)],
        compiler_params=_cparams(("arbitrary", "arbitrary")),
        name="diff_attn",
    )(lam_p, gain_b, qt, k, vt)


N_PAIRS = DIL_W // LANES
DIL_UNROLL = 2


def _dil_block(qs, ks, vs, biases):
    nq = DIL_BLOCK
    low = lax.broadcasted_iota(jnp.int32, (nq, LANES), 1) < DIL_DH
    scores = []
    for qp, kp, bias in zip(qs, ks, biases):
        q2 = jnp.concatenate([jnp.where(low, qp, 0.0), jnp.where(low, 0.0, qp)], axis=0).astype(BF16)
        scores.append(lax.dot_general(q2, kp.astype(BF16), (((1,), (1,)), ((), ())),
                                      preferred_element_type=F32) + bias)
    out = []
    for s, vp in zip(scores, vs):
        m = jnp.max(s, axis=1, keepdims=True)
        p = jnp.exp(s - m)
        den = jnp.sum(p, axis=1, keepdims=True)
        num = jnp.dot(p.astype(BF16), vp.astype(BF16), preferred_element_type=F32)
        out.append(tuple(jnp.where(low, t[0:nq], t[nq:2 * nq]) for t in (num, den, m)))
    return out


def _dil_merge(a, b):
    (n1, d1, m1), (n2, d2, m2) = a, b
    m = jnp.maximum(m1, m2)
    w1 = jnp.exp(m1 - m)
    w2 = jnp.exp(m2 - m)
    return n1 * w1 + n2 * w2, d1 * w1 + d2 * w2, m


def _dil_attn_kernel(q_ref, k_ref, v_ref, o_ref, qf_ref, kk_ref, vv_ref, num_ref, den_ref, max_ref, bias_ref):
    c = pl.program_id(1)
    ch = q_ref.shape[0]

    iq = lax.broadcasted_iota(jnp.int32, (2 * DIL_BLOCK, 2 * DIL_BLOCK), 0) % DIL_BLOCK
    jk = lax.broadcasted_iota(jnp.int32, (2 * DIL_BLOCK, 2 * DIL_BLOCK), 1)
    band = (jk >= iq) & (jk <= iq + DIL_BLOCK)
    bias_ref[0] = jnp.where(band, 0.0, -jnp.inf)
    bias_ref[1] = jnp.where(band & (jk >= DIL_BLOCK), 0.0, -jnp.inf)

    @pl.when(c == 0)
    def _():
        kk_ref[:, 0:ch, :] = jnp.zeros((N_PAIRS, ch, LANES), F32)
        vv_ref[:, 0:ch, :] = jnp.zeros((N_PAIRS, ch, LANES), F32)

    for pr in range(N_PAIRS):
        sl = slice(pr * LANES, (pr + 1) * LANES)
        qf_ref[pr] = q_ref[:, sl].astype(F32)
        kk_ref[pr, ch:2 * ch, :] = k_ref[:, sl].astype(F32)
        vv_ref[pr, ch:2 * ch, :] = v_ref[:, sl].astype(F32)
    first_chunk = c == 0

    def blocks(specs):
        qs, ks, vs, biases = [], [], [], []
        for rows, keys, first_block in specs:
            bias = bias_ref[first_block.astype(jnp.int32)]
            for pr in range(N_PAIRS):
                qs.append(qf_ref[pr, rows, :])
                ks.append(kk_ref[pr, keys, :])
                vs.append(vv_ref[pr, keys, :])
                biases.append(bias)
        res = _dil_block(qs, ks, vs, biases)
        return [res[t * N_PAIRS:(t + 1) * N_PAIRS] for t in range(len(specs))]

    def merged(pr, rows, new):
        return _dil_merge((num_ref[pr, rows, :], den_ref[pr, rows, :], max_ref[pr, rows, :]), new)

    def store(pr, rows, ndm):
        num_ref[pr, rows, :] = ndm[0]
        den_ref[pr, rows, :] = ndm[1]
        max_ref[pr, rows, :] = ndm[2]

    d16 = DIL_PATTERNS[2][1]

    def body16(rr, carry):
        specs = [(pl.ds(DIL_UNROLL * rr + u, DIL_BLOCK, stride=d16),
                  pl.ds(DIL_UNROLL * rr + u, 2 * DIL_BLOCK, stride=d16), first_chunk) for u in range(DIL_UNROLL)]
        for (rows, _, _), res in zip(specs, blocks(specs)):
            for pr, new in enumerate(res):
                store(pr, rows, new)
        return carry

    lax.fori_loop(0, d16 // DIL_UNROLL, body16, 0)

    d4 = DIL_PATTERNS[1][1]
    blocks4 = ch // (d4 * DIL_BLOCK)

    def body4(tt, carry):
        nb = (DIL_UNROLL * tt) // d4
        first_block = jnp.logical_and(first_chunk, nb == 0)
        specs = []
        for u in range(DIL_UNROLL):
            q0 = (DIL_UNROLL * tt) % d4 + u + nb * (d4 * DIL_BLOCK)
            specs.append((pl.ds(q0, DIL_BLOCK, stride=d4),
                          pl.ds(ch + q0 - d4 * DIL_BLOCK, 2 * DIL_BLOCK, stride=d4), first_block))
        for (rows, _, _), res in zip(specs, blocks(specs)):
            for pr, new in enumerate(res):
                store(pr, rows, merged(pr, rows, new))
        return carry

    lax.fori_loop(0, d4 * blocks4 // DIL_UNROLL, body4, 0)

    def body1(nn, carry):
        specs = []
        for u in range(DIL_UNROLL):
            nb = DIL_UNROLL * nn + u
            q0 = pl.multiple_of(nb * DIL_BLOCK, DIL_BLOCK)
            specs.append((pl.ds(q0, DIL_BLOCK),
                          pl.ds(pl.multiple_of(ch + q0 - DIL_BLOCK, DIL_BLOCK), 2 * DIL_BLOCK),
                          jnp.logical_and(first_chunk, nb == 0)))
        for (rows, _, _), res in zip(specs, blocks(specs)):
            for pr, new in enumerate(res):
                n, d, _ = merged(pr, rows, new)
                o_ref[rows, pr * LANES:(pr + 1) * LANES] = (n / d).astype(o_ref.dtype)
        return carry

    lax.fori_loop(0, ch // DIL_BLOCK // DIL_UNROLL, body1, 0)

    kk_ref[:, 0:ch, :] = kk_ref[:, ch:2 * ch, :]
    vv_ref[:, 0:ch, :] = vv_ref[:, ch:2 * ch, :]


def _dil_attn(q, k, v):
    B, S, W = q.shape
    ch = DIL_CH
    blk = pl.BlockSpec((None, ch, W), lambda b, c: (b, c, 0))
    return pl.pallas_call(
        _dil_attn_kernel,
        grid=(B, S // ch),
        in_specs=[blk, blk, blk],
        out_specs=blk,
        out_shape=jax.ShapeDtypeStruct((B, S, W), BF16),
        scratch_shapes=[pltpu.VMEM((N_PAIRS, ch, LANES), F32), pltpu.VMEM((N_PAIRS, 2 * ch, LANES), F32),
                        pltpu.VMEM((N_PAIRS, 2 * ch, LANES), F32), pltpu.VMEM((N_PAIRS, ch, LANES), F32),
                        pltpu.VMEM((N_PAIRS, ch, LANES), F32), pltpu.VMEM((N_PAIRS, ch, LANES), F32),
                        pltpu.VMEM((2, 2 * DIL_BLOCK, 2 * DIL_BLOCK), F32)],
        compiler_params=_cparams(("arbitrary", "arbitrary")),
        name="dil_attn",
    )(q, k, v)


def _shift_rows(x, tail8, s):
    n = x.shape[0]
    row = lax.broadcasted_iota(jnp.int32, x.shape, 0)
    head = pltpu.repeat(pltpu.roll(tail8, s, 0), n // SUBLANES, axis=0)
    return jnp.where(row < s, head, pltpu.roll(x, s, 0))


def _expand_heads(v, n_heads, width):
    rows = v.shape[0]
    lane = lax.broadcasted_iota(jnp.int32, (rows, n_heads * width), 1)
    out = jnp.zeros((rows, n_heads * width), F32)
    for h in range(n_heads):
        out = jnp.where(lane // width == h, v[:, h:h + 1], out)
    return out


def _ssd_kernel(z_ref, xbc_ref, dt_ref, cw_ref, cb_ref, dtb_ref, alog_ref, dvec_ref, gain_ref, tri_ref,
                o_ref, state_ref, tail_ref):
    c = pl.program_id(1)
    lc = xbc_ref.shape[0]
    L = SSD_CHUNK

    @pl.when(c == 0)
    def _():
        state_ref[...] = jnp.zeros(state_ref.shape, F32)
        tail_ref[...] = jnp.zeros(tail_ref.shape, F32)

    xbc = xbc_ref[...]
    tail8 = tail_ref[...]
    cw = cw_ref[...]
    acc = xbc * cw[SSD_CONV - 1:SSD_CONV, :] + cb_ref[...]
    for kk in range(SSD_CONV - 1):
        acc = acc + _shift_rows(xbc, tail8, SSD_CONV - 1 - kk) * cw[kk:kk + 1, :]
    tail_ref[...] = xbc[lc - SUBLANES:lc, :]
    xbc_c = _silu(acc)

    xs_all = xbc_c[:, 0:SSD_INNER]
    dt_raw = dt_ref[...] + dtb_ref[...]
    dt_all = jnp.maximum(dt_raw, 0.0) + jnp.log(1.0 + jnp.exp(-jnp.abs(dt_raw)))
    a_all = dt_all * (-jnp.exp(alog_ref[...]))
    tri = tri_ref[...]
    lane_in = lax.broadcasted_iota(jnp.int32, (L, SSD_INNER), 1)
    grp_w = SSD_INNER // SSD_GROUPS
    ii = lax.broadcasted_iota(jnp.int32, (L, L), 0)
    jj = lax.broadcasted_iota(jnp.int32, (L, L), 1)
    causal = jj <= ii

    for ck in range(lc // L):
        rs = slice(ck * L, (ck + 1) * L)
        xs = xs_all[rs, :]
        a = a_all[rs, :]
        hi, mid, lo = _split3(a)
        acs = (jnp.dot(tri, hi, preferred_element_type=F32) + jnp.dot(tri, mid, preferred_element_type=F32)
               + jnp.dot(tri, lo, preferred_element_type=F32))
        acs_t = acs.T
        dt_e = _expand_heads(dt_all[rs, :], SSD_HEADS, SSD_P)
        acs_e = _expand_heads(acs, SSD_HEADS, SSD_P)
        last_e = acs_e[L - 1:L, :]
        xdt = xs * dt_e
        xdt_b = xdt.astype(BF16)
        xdtd_b = (xdt * jnp.exp(last_e - acs_e)).astype(BF16)
        prev = state_ref[...]
        prev_b = prev.astype(BF16)
        y = xs * dvec_ref[...]
        new_state = prev * jnp.exp(last_e)
        y_off = jnp.zeros((L, SSD_INNER), F32)
        for g in range(SSD_GROUPS):
            bg = xbc_c[rs, SSD_INNER + g * SSD_N:SSD_INNER + (g + 1) * SSD_N]
            cg = xbc_c[rs, SSD_INNER + (SSD_GROUPS + g) * SSD_N:SSD_INNER + (SSD_GROUPS + g + 1) * SSD_N]
            bg_b = bg.astype(BF16)
            cg_b = cg.astype(BF16)
            in_grp = lane_in // grp_w == g
            cb = lax.dot_general(cg_b, bg_b, (((1,), (1,)), ((), ())), preferred_element_type=F32)
            for hh in range(SSD_HEADS // SSD_GROUPS):
                hd = g * (SSD_HEADS // SSD_GROUPS) + hh
                seg = acs[:, hd:hd + 1] - acs_t[hd:hd + 1, :]
                w = jnp.where(causal, cb * jnp.exp(seg), 0.0).astype(BF16)
                y = y + jnp.dot(w, jnp.where(lane_in // SSD_P == hd, xdt_b, jnp.zeros_like(xdt_b)),
                                preferred_element_type=F32)
            zero_b = jnp.zeros_like(xdtd_b)
            new_state = new_state + jnp.dot(bg.T.astype(BF16), jnp.where(in_grp, xdtd_b, zero_b),
                                            preferred_element_type=F32)
            y_off = y_off + jnp.dot(cg_b, jnp.where(in_grp, prev_b, jnp.zeros_like(prev_b)),
                                    preferred_element_type=F32)
        state_ref[...] = new_state
        y = y + y_off * jnp.exp(acs_e)
        y = y * _silu(z_ref[rs, :].astype(F32))
        ysq = y * y
        in0 = lane_in < grp_w
        ms0 = jnp.sum(jnp.where(in0, ysq, 0.0), axis=1, keepdims=True) * (1.0 / grp_w)
        ms1 = jnp.sum(jnp.where(in0, 0.0, ysq), axis=1, keepdims=True) * (1.0 / grp_w)
        r = jnp.where(in0, lax.rsqrt(ms0 + EPS), lax.rsqrt(ms1 + EPS))
        o_ref[rs, :] = ((y * r) * gain_ref[...]).astype(o_ref.dtype)


def _ssd(z, xbc, dt, cw, cb, dtb, alog, dvec, gain, tri):
    B, S, _ = z.shape
    lc = SSD_LC
    tok = lambda w: pl.BlockSpec((None, lc, w), lambda b, c: (b, c, 0))
    consts = (cw, cb, dtb, alog, dvec, gain, tri)
    return pl.pallas_call(
        _ssd_kernel,
        grid=(B, S // lc),
        in_specs=[tok(SSD_INNER), tok(SSD_XBC), tok(LANES)] + [_const_spec(a.shape) for a in consts],
        out_specs=tok(SSD_INNER),
        out_shape=jax.ShapeDtypeStruct((B, S, SSD_INNER), BF16),
        scratch_shapes=[pltpu.VMEM((SSD_N, SSD_INNER), F32), pltpu.VMEM((SUBLANES, SSD_XBC), F32)],
        compiler_params=_cparams(("arbitrary", "arbitrary")),
        name="ssd",
    )(z, xbc, dt, *consts)


def _rms(x, gain):
    ms = jnp.mean(x * x, axis=-1, keepdims=True)
    return (x * lax.rsqrt(ms + EPS)) * gain


def _out_ffn_kernel(x_ref, od_ref, ol_ref, os_ref, wo_d_ref, wo_l_ref, wo_s_ref, g_mix_ref, g_pre_ref,
                    up_ref, cw_ref, cb_ref, down_ref, g_post_ref, o_ref, tail_ref, *, tiles_per_seq):
    t = pl.program_id(0)
    tm = x_ref.shape[0]

    @pl.when(t % tiles_per_seq == 0)
    def _():
        tail_ref[...] = jnp.zeros(tail_ref.shape, F32)

    mix = (jnp.dot(od_ref[...], wo_d_ref[...], preferred_element_type=F32)
           + jnp.dot(ol_ref[...], wo_l_ref[...], preferred_element_type=F32)
           + jnp.dot(os_ref[...], wo_s_ref[...], preferred_element_type=F32))
    x1 = x_ref[...] + _rms(mix, g_mix_ref[...])
    h = _rms(x1, g_pre_ref[...]).astype(BF16)

    def up(ci):
        return [jnp.dot(h, up_ref[:, off + ci * FF_CHUNK:off + (ci + 1) * FF_CHUNK], preferred_element_type=F32)
                for off in (0, D_FF)]

    n_chunks = D_FF // FF_CHUNK
    f_acc = jnp.zeros((tm, D_MODEL), F32)
    u_next = up(0)
    for ci in range(n_chunks):
        u_pair = u_next
        if ci + 1 < n_chunks:
            u_next = up(ci + 1)
        halves = []
        for u, off in zip(u_pair, (0, D_FF)):
            cs = slice(off + ci * FF_CHUNK, off + (ci + 1) * FF_CHUNK)
            tail8 = tail_ref[:, cs]
            cw = cw_ref[:, cs]
            conv = u * cw[FFN_CONV - 1:FFN_CONV, :] + cb_ref[:, cs]
            for kk in range(FFN_CONV - 1):
                conv = conv + _shift_rows(u, tail8, FFN_CONV - 1 - kk) * cw[kk:kk + 1, :]
            tail_ref[:, cs] = u[tm - SUBLANES:tm, :]
            halves.append(conv)
        f = (_silu(halves[0]) * halves[1]).astype(BF16)
        f_acc = f_acc + jnp.dot(f, down_ref[ci * FF_CHUNK:(ci + 1) * FF_CHUNK, :], preferred_element_type=F32)
    o_ref[...] = x1 + _rms(f_acc, g_post_ref[...])


def _out_ffn(x2d, od, ol, os_, wo_d, wo_l, wo_s, g_mix, g_pre, up, cw, cb, down, g_post, tiles_per_seq):
    T, D = x2d.shape
    tm = TM_FFN
    tok = lambda w: pl.BlockSpec((tm, w), lambda t: (t, 0))
    consts = (wo_d, wo_l, wo_s, g_mix, g_pre, up, cw, cb, down, g_post)
    return pl.pallas_call(
        functools.partial(_out_ffn_kernel, tiles_per_seq=tiles_per_seq),
        grid=(T // tm,),
        in_specs=[tok(D), tok(DIFF_W), tok(DIL_W), tok(SSD_INNER)]
                 + [pl.BlockSpec(a.shape, lambda t: (0, 0), pipeline_mode=pl.Buffered(1)) for a in consts],
        out_specs=tok(D),
        out_shape=jax.ShapeDtypeStruct((T, D), F32),
        scratch_shapes=[pltpu.VMEM((SUBLANES, 2 * D_FF), F32)],
        compiler_params=_cparams(("arbitrary",)),
        name="out_ffn",
    )(x2d, od, ol, os_, *consts)


def _rope_tables(S, half):
    inv_freq = jnp.exp(-math.log(ROPE_THETA) * jnp.arange(half, dtype=F32) / half)
    ang = jnp.arange(S, dtype=F32)[:, None] * inv_freq[None, :]
    cos, sin = jnp.cos(ang), jnp.sin(ang)
    zero = jnp.zeros_like(sin)
    reps = LANES // (2 * half)
    cos_t = jnp.tile(jnp.concatenate([cos, cos], axis=1), (1, reps))
    sin_up = jnp.tile(jnp.concatenate([-sin, zero], axis=1), (1, reps))
    sin_dn = jnp.tile(jnp.concatenate([zero, sin], axis=1), (1, reps))
    return cos_t, sin_up, sin_dn


def _rope_tables_t(S, half):
    inv_freq = jnp.exp(-math.log(ROPE_THETA) * jnp.arange(half, dtype=F32) / half)
    ang = jnp.arange(S, dtype=F32)[:, None] * inv_freq[None, :]
    return jnp.cos(ang).T, jnp.sin(ang).T


def kernel(x, pre_mix_norm, w_in, diff_lambda, diff_head_norm, ssd_conv_w, ssd_conv_b, ssd_dt_bias, ssd_A_log,
           ssd_D, ssd_norm, w_out, post_mix_norm, pre_ffn_norm, ffn_up, ffn_conv_w, ffn_conv_b, ffn_down,
           post_ffn_norm):
    B, S, D = x.shape
    depth = w_in.shape[0]
    assert D == D_MODEL and S % DIL_CH == 0 and S % TM_IN == 0 and S % BQ == 0 and S % TM_FFN == 0
    tables = _rope_tables_t(S, DIFF_QK // 2) + _rope_tables(S, DIFF_QK // 2) + _rope_tables(S, DIL_DH // 2)
    tri = (lax.broadcasted_iota(jnp.int32, (SSD_CHUNK, SSD_CHUNK), 1)
           <= lax.broadcasted_iota(jnp.int32, (SSD_CHUNK, SSD_CHUNK), 0)).astype(BF16)
    offs = [0]
    for sz in IN_SIZES:
        offs.append(offs[-1] + sz)

    def pad_lanes(v):
        return jnp.pad(v, ((0, 0), (0, LANES - v.shape[1])))

    for layer in range(depth):
        lambda_init = 0.8 - 0.6 * math.exp(-0.3 * layer)
        wl = w_in[layer].astype(BF16)
        w_a = jnp.concatenate([wl[:, offs[1]:offs[2]], wl[:, offs[3]:offs[5]]], axis=1)
        w_b = jnp.pad(wl[:, offs[5]:], ((0, 0), (0, LANES - SSD_HEADS)))
        ws = [wl[:, offs[0]:offs[1]].T, w_a, w_b, wl[:, offs[2]:offs[3]].T]
        dqt, dk, dvt, lq, lk, lv, z, xbc, dt = _in_proj(x, pre_mix_norm[layer][None, :], tables, ws)

        o_diff = _diff_attn(dqt, dk, dvt, diff_lambda[layer],
                            jnp.broadcast_to(diff_head_norm[layer][:, None], (DIFF_V, LANES)), lambda_init)
        o_dil = _dil_attn(lq, lk, lv)
        o_ssd = _ssd(z, xbc, dt, ssd_conv_w[layer], ssd_conv_b[layer][None, :],
                     pad_lanes(ssd_dt_bias[layer][None, :]), pad_lanes(ssd_A_log[layer][None, :]),
                     jnp.repeat(ssd_D[layer], SSD_P)[None, :], ssd_norm[layer][None, :], tri)

        wo = w_out[layer].astype(BF16)
        x2d = _out_ffn(
            x.reshape(B * S, D), o_diff.reshape(B * S, DIFF_W), o_dil.reshape(B * S, DIL_W),
            o_ssd.reshape(B * S, SSD_INNER),
            wo[0:DIFF_W], wo[DIFF_W:DIFF_W + DIL_W], wo[DIFF_W + DIL_W:],
            post_mix_norm[layer][None, :], pre_ffn_norm[layer][None, :],
            ffn_up[layer].astype(BF16), ffn_conv_w[layer], ffn_conv_b[layer][None, :],
            ffn_down[layer].astype(BF16), post_ffn_norm[layer][None, :], S // TM_FFN)
        x = x2d.reshape(B, S, D)
    return x
```

```python
import functools
import math

import jax
import jax.numpy as jnp
from jax import lax
from jax.experimental import pallas as pl
from jax.experimental.pallas import tpu as pltpu

F32 = jnp.float32
BF16 = jnp.bfloat16

LANES = 128
SUBLANES = 8
VMEM_LIMIT = 56 * 1024 * 1024

D_MODEL = 1024
DIFF_HEADS = 4
DIFF_QK = 32
DIFF_V = 64
DIFF_W = DIFF_HEADS * DIFF_V
DIL_HEADS = 6
DIL_DH = 64
DIL_W = DIL_HEADS * DIL_DH
DIL_PATTERNS = ((128, 1), (512, 4), (2048, 16))
DIL_BLOCK = 128
SSD_HEADS = 6
SSD_P = 64
SSD_GROUPS = 2
SSD_N = 128
SSD_CONV = 4
SSD_CHUNK = 128
SSD_INNER = SSD_HEADS * SSD_P
SSD_XBC = SSD_INNER + 2 * SSD_GROUPS * SSD_N
D_FF = 2816
FFN_CONV = 3
ROPE_THETA = 10000.0
EPS = 1e-6
LOG2E = 1.4426950408889634

IN_SIZES = (DIFF_W, DIFF_W, DIFF_W, DIL_W, DIL_W, DIL_W, SSD_INNER, SSD_XBC, SSD_HEADS)

TM_IN = 512
BK = 256
BQ = 2 * BK
DIL_CH = 2048
SSD_LC = 512
TM_FFN = 256
FF_CHUNK = 256
FF_AHEAD = 2


def _cparams(sem):
    return pltpu.CompilerParams(dimension_semantics=sem, vmem_limit_bytes=VMEM_LIMIT)


def _const_spec(shape):
    nd = len(shape)
    return pl.BlockSpec(shape, lambda *_: (0,) * nd)


def _silu(x):
    return x * (1.0 / (1.0 + jnp.exp(-x)))


def _split3(x):
    hi = x.astype(BF16)
    r = x - hi.astype(F32)
    mid = r.astype(BF16)
    lo = (r - mid.astype(F32)).astype(BF16)
    return hi, mid, lo


def _rope_lanes(y, cos, sin_up, sin_dn, half):
    outs = []
    for j in range(y.shape[1] // LANES):
        yc = y[:, j * LANES:(j + 1) * LANES]
        up = pltpu.roll(yc, LANES - half, 1)
        dn = pltpu.roll(yc, half, 1)
        outs.append(yc * cos + up * sin_up + dn * sin_dn)
    return jnp.concatenate(outs, axis=1)


def _rope_rows(yt, cos_t, sin_t, half):
    outs = []
    for g in range(yt.shape[0] // (2 * half)):
        x1 = yt[2 * half * g:2 * half * g + half, :]
        x2 = yt[2 * half * g + half:2 * half * (g + 1), :]
        outs += [x1 * cos_t - x2 * sin_t, x2 * cos_t + x1 * sin_t]
    return jnp.concatenate(outs, axis=0)


def _in_proj_kernel(x_ref, g_ref, cdt_ref, sdt_ref, cd_ref, sud_ref, sdd_ref, cl_ref, sul_ref, sdl_ref,
                    wqt_ref, wa_ref, wb_ref, wvt_ref,
                    oqt_ref, ok_ref, ovt_ref, olq_ref, olk_ref, olv_ref, oz_ref, oxbc_ref, odt_ref):
    x = x_ref[...]
    ms = jnp.mean(x * x, axis=-1, keepdims=True)
    h = ((x * lax.rsqrt(ms + EPS)) * g_ref[...]).astype(BF16)

    def proj(w_ref):
        return jnp.dot(h, w_ref[...], preferred_element_type=F32)

    def proj_t(wt_ref):
        return lax.dot_general(wt_ref[...], h, (((1,), (1,)), ((), ())), preferred_element_type=F32)

    r_qt = proj_t(wqt_ref)
    r_a = proj(wa_ref)
    r_b = proj(wb_ref)
    r_vt = proj_t(wvt_ref)

    qt = _rope_rows(r_qt, cdt_ref[...], sdt_ref[...], DIFF_QK // 2)
    oqt_ref[...] = (qt * (DIFF_QK ** -0.5 * LOG2E)).astype(BF16)
    cd, sud, sdd = cd_ref[...], sud_ref[...], sdd_ref[...]
    a0, a1, a2 = DIFF_W, DIFF_W + DIL_W, DIFF_W + 2 * DIL_W
    ok_ref[...] = _rope_lanes(r_a[:, 0:a0], cd, sud, sdd, DIFF_QK // 2).astype(BF16)
    cl, sul, sdl = cl_ref[...], sul_ref[...], sdl_ref[...]
    lq = _rope_lanes(r_a[:, a0:a1], cl, sul, sdl, DIL_DH // 2)
    olq_ref[...] = (lq * (DIL_DH ** -0.5)).astype(BF16)
    olk_ref[...] = _rope_lanes(r_a[:, a1:a2], cl, sul, sdl, DIL_DH // 2).astype(BF16)
    b0, b1, b2 = DIL_W, DIL_W + SSD_INNER, DIL_W + SSD_INNER + SSD_XBC
    olv_ref[...] = r_b[:, 0:b0].astype(BF16)
    oz_ref[...] = r_b[:, b0:b1].astype(BF16)
    oxbc_ref[...] = r_b[:, b1:b2]
    odt_ref[...] = r_b[:, b2:b2 + LANES]
    ovt_ref[...] = r_vt.astype(BF16)


def _in_proj(x, gain, tables, ws):
    B, S, D = x.shape
    tm = TM_IN
    grid = (S // tm, B)
    tok = lambda w: pl.BlockSpec((None, tm, w), lambda s, b: (b, s, 0))
    tok_t = lambda w: pl.BlockSpec((None, w, tm), lambda s, b: (b, 0, s))
    tab = pl.BlockSpec((tm, LANES), lambda s, b: (s, 0))
    tab_t = pl.BlockSpec((DIFF_QK // 2, tm), lambda s, b: (0, s))
    out_widths = (DIFF_W, DIFF_W, DIFF_W, DIL_W, DIL_W, DIL_W, SSD_INNER, SSD_XBC, LANES)
    out_dtypes = (BF16, BF16, BF16, BF16, BF16, BF16, BF16, F32, F32)
    transposed = (True, False, True) + (False,) * 6
    return pl.pallas_call(
        _in_proj_kernel,
        grid=grid,
        in_specs=[tok(D), _const_spec((1, D))] + [tab_t] * 2 + [tab] * 6 + [_const_spec(w.shape) for w in ws],
        out_specs=[tok_t(w) if t else tok(w) for w, t in zip(out_widths, transposed)],
        out_shape=[jax.ShapeDtypeStruct((B, w, S) if t else (B, S, w), dt)
                   for w, dt, t in zip(out_widths, out_dtypes, transposed)],
        compiler_params=_cparams(("arbitrary", "arbitrary")),
        name="in_proj",
    )(x, gain, *tables, *ws)


N_SM = 2 * DIFF_HEADS
ACC_ROWS = DIFF_V + 16
AHEAD = 0


def _diff_attn_kernel(lam_ref, gain_ref, qt_ref, k_ref, vt_ref, o_ref, qs_ref, m_ref, acc_ref, sa_ref, sb_ref, *,
                      lambda_init):
    i = pl.program_id(1)
    bq = qt_ref.shape[1]
    qt = qt_ref[...]
    feat = lax.broadcasted_iota(jnp.int32, (DIFF_W, bq), 0)
    zero = jnp.zeros_like(qt)
    for g in range(N_SM):
        qs_ref[:, g * bq:(g + 1) * bq] = jnp.where(feat // DIFF_QK == g, qt, zero)
    m_ref[...] = jnp.full(m_ref.shape, -jnp.inf, F32)
    acc_ref[...] = jnp.zeros(acc_ref.shape, F32)
    bk = sa_ref.shape[0]
    ones = jnp.ones((ACC_ROWS - DIFF_V, bk), BF16)

    def key_tile(j):
        return k_ref[pl.ds(pl.multiple_of(j * bk, bk), bk), :]

    def scores_into(dst_ref, kj, g):
        cols = slice(g * bq, (g + 1) * bq)
        dst_ref[:, cols] = jnp.dot(kj, qs_ref[:, cols], preferred_element_type=F32)

    def step(src_ref, j, diag, dst_ref):
        vtj = vt_ref[:, pl.ds(pl.multiple_of(j * bk, bk), bk)]
        masked = diag is not None
        if masked:
            key = lax.broadcasted_iota(jnp.int32, (bk, bq), 0) + diag
            qry = lax.broadcasted_iota(jnp.int32, (bk, bq), 1)
            visible = key <= qry
        if dst_ref is not None:
            kn = key_tile(j + 1)
            for g in range(AHEAD):
                scores_into(dst_ref, kn, g)
        m_all = m_ref[...]
        new_m, new_acc = [], []
        for g in range(N_SM):
            s = src_ref[:, g * bq:(g + 1) * bq]
            if masked:
                s = jnp.where(visible, s, -jnp.inf)
            m_prev = m_all[:, g * bq:(g + 1) * bq]
            m_new = jnp.maximum(m_prev, jnp.max(s, axis=0, keepdims=True))
            alpha = jnp.exp2(m_prev - m_new)
            p = jnp.exp2(s - m_new).astype(BF16)
            if dst_ref is not None and g + AHEAD < N_SM:
                scores_into(dst_ref, kn, g + AHEAD)
            hd = g // 2
            lhs = jnp.concatenate([vtj[hd * DIFF_V:(hd + 1) * DIFF_V, :], ones], axis=0)
            pv = jnp.dot(lhs, p, preferred_element_type=F32)
            new_m.append(m_new)
            new_acc.append(acc_ref[g] * alpha + pv)
        m_ref[...] = jnp.concatenate(new_m, axis=1)
        for g in range(N_SM):
            acc_ref[g] = new_acc[g]

    k0 = key_tile(0)
    for g in range(N_SM):
        scores_into(sa_ref, k0, g)

    def body(jj, carry):
        step(sa_ref, 2 * jj, None, sb_ref)
        step(sb_ref, 2 * jj + 1, None, sa_ref)
        return carry

    lax.fori_loop(0, i, body, 0)
    step(sa_ref, 2 * i, 0, sb_ref)
    step(sb_ref, 2 * i + 1, bk, None)

    lam_p = lam_ref[...]
    s1 = jnp.sum(lam_p[0:1, :] * lam_p[1:2, :], axis=1, keepdims=True)
    s2 = jnp.sum(lam_p[2:3, :] * lam_p[3:4, :], axis=1, keepdims=True)
    lam = jnp.exp(s1) - jnp.exp(s2) + lambda_init
    gain = pltpu.repeat(gain_ref[...], bq // LANES, axis=1)
    outs = []
    for hd in range(DIFF_HEADS):
        a1 = acc_ref[2 * hd]
        a2 = acc_ref[2 * hd + 1]
        o1 = a1[0:DIFF_V, :] / a1[DIFF_V:DIFF_V + 1, :]
        o2 = a2[0:DIFF_V, :] / a2[DIFF_V:DIFF_V + 1, :]
        d = o1 - lam * o2
        ms = jnp.mean(d * d, axis=0, keepdims=True)
        outs.append(((d * lax.rsqrt(ms + EPS)) * gain) * (1.0 - lambda_init))
    o_ref[...] = jnp.concatenate(outs, axis=0).T.astype(o_ref.dtype)


def _diff_attn(qt, k, vt, lam_p, gain_b, lambda_init):
    B, S, W = k.shape
    grid = (B, S // BQ)
    return pl.pallas_call(
        functools.partial(_diff_attn_kernel, lambda_init=lambda_init),
        grid=grid,
        in_specs=[_const_spec(lam_p.shape), _const_spec(gain_b.shape),
                  pl.BlockSpec((None, W, BQ), lambda b, i: (b, 0, i)),
                  pl.BlockSpec((None, S, W), lambda b, i: (b, 0, 0)),
                  pl.BlockSpec((None, W, S), lambda b, i: (b, 0, 0))],
        out_specs=pl.BlockSpec((None, BQ, W), lambda b, i: (b, i, 0)),
        out_shape=jax.ShapeDtypeStruct((B, S, W), BF16),
        scratch_shapes=[pltpu.VMEM((W, N_SM * BQ), BF16),
                        pltpu.VMEM((1, N_SM * BQ), F32),
                        pltpu.VMEM((N_SM, ACC_ROWS, BQ), F32),
                        pltpu.VMEM((BK, N_SM * BQ), F32),
                        pltpu.VMEM((BK, N_SM * BQ), F32)],
        compiler_params=_cparams(("arbitrary", "arbitrary")),
        name="diff_attn",
    )(lam_p, gain_b, qt, k, vt)


N_PAIRS = DIL_W // LANES
DIL_UNROLL = 2


def _dil_block(qs, ks, vs, biases):
    nq = DIL_BLOCK
    low = lax.broadcasted_iota(jnp.int32, (nq, LANES), 1) < DIL_DH
    scores = []
    for qp, kp, bias in zip(qs, ks, biases):
        q2 = jnp.concatenate([jnp.where(low, qp, 0.0), jnp.where(low, 0.0, qp)], axis=0).astype(BF16)
        scores.append(lax.dot_general(q2, kp.astype(BF16), (((1,), (1,)), ((), ())),
                                      preferred_element_type=F32) + bias)
    out = []
    for s, vp in zip(scores, vs):
        m = jnp.max(s, axis=1, keepdims=True)
        p = jnp.exp(s - m)
        den = jnp.sum(p, axis=1, keepdims=True)
        num = jnp.dot(p.astype(BF16), vp.astype(BF16), preferred_element_type=F32)
        out.append(tuple(jnp.where(low, t[0:nq], t[nq:2 * nq]) for t in (num, den, m)))
    return out


def _dil_merge(a, b):
    (n1, d1, m1), (n2, d2, m2) = a, b
    m = jnp.maximum(m1, m2)
    w1 = jnp.exp(m1 - m)
    w2 = jnp.exp(m2 - m)
    return n1 * w1 + n2 * w2, d1 * w1 + d2 * w2, m


def _dil_attn_kernel(q_ref, k_ref, v_ref, o_ref, qf_ref, kk_ref, vv_ref, num_ref, den_ref, max_ref, bias_ref):
    c = pl.program_id(1)
    ch = q_ref.shape[0]

    iq = lax.broadcasted_iota(jnp.int32, (2 * DIL_BLOCK, 2 * DIL_BLOCK), 0) % DIL_BLOCK
    jk = lax.broadcasted_iota(jnp.int32, (2 * DIL_BLOCK, 2 * DIL_BLOCK), 1)
    band = (jk >= iq) & (jk <= iq + DIL_BLOCK)
    bias_ref[0] = jnp.where(band, 0.0, -jnp.inf)
    bias_ref[1] = jnp.where(band & (jk >= DIL_BLOCK), 0.0, -jnp.inf)

    @pl.when(c == 0)
    def _():
        kk_ref[:, 0:ch, :] = jnp.zeros((N_PAIRS, ch, LANES), F32)
        vv_ref[:, 0:ch, :] = jnp.zeros((N_PAIRS, ch, LANES), F32)

    for pr in range(N_PAIRS):
        sl = slice(pr * LANES, (pr + 1) * LANES)
        qf_ref[pr] = q_ref[:, sl].astype(F32)
        kk_ref[pr, ch:2 * ch, :] = k_ref[:, sl].astype(F32)
        vv_ref[pr, ch:2 * ch, :] = v_ref[:, sl].astype(F32)
    first_chunk = c == 0

    def blocks(specs):
        qs, ks, vs, biases = [], [], [], []
        for rows, keys, first_block in specs:
            bias = bias_ref[first_block.astype(jnp.int32)]
            for pr in range(N_PAIRS):
                qs.append(qf_ref[pr, rows, :])
                ks.append(kk_ref[pr, keys, :])
                vs.append(vv_ref[pr, keys, :])
                biases.append(bias)
        res = _dil_block(qs, ks, vs, biases)
        return [res[t * N_PAIRS:(t + 1) * N_PAIRS] for t in range(len(specs))]

    def merged(pr, rows, new):
        return _dil_merge((num_ref[pr, rows, :], den_ref[pr, rows, :], max_ref[pr, rows, :]), new)

    def store(pr, rows, ndm):
        num_ref[pr, rows, :] = ndm[0]
        den_ref[pr, rows, :] = ndm[1]
        max_ref[pr, rows, :] = ndm[2]

    d16 = DIL_PATTERNS[2][1]

    def body16(rr, carry):
        specs = [(pl.ds(DIL_UNROLL * rr + u, DIL_BLOCK, stride=d16),
                  pl.ds(DIL_UNROLL * rr + u, 2 * DIL_BLOCK, stride=d16), first_chunk) for u in range(DIL_UNROLL)]
        for (rows, _, _), res in zip(specs, blocks(specs)):
            for pr, new in enumerate(res):
                store(pr, rows, new)
        return carry

    lax.fori_loop(0, d16 // DIL_UNROLL, body16, 0)

    d4 = DIL_PATTERNS[1][1]
    blocks4 = ch // (d4 * DIL_BLOCK)

    def body4(tt, carry):
        nb = (DIL_UNROLL * tt) // d4
        first_block = jnp.logical_and(first_chunk, nb == 0)
        specs = []
        for u in range(DIL_UNROLL):
            q0 = (DIL_UNROLL * tt) % d4 + u + nb * (d4 * DIL_BLOCK)
            specs.append((pl.ds(q0, DIL_BLOCK, stride=d4),
                          pl.ds(ch + q0 - d4 * DIL_BLOCK, 2 * DIL_BLOCK, stride=d4), first_block))
        for (rows, _, _), res in zip(specs, blocks(specs)):
            for pr, new in enumerate(res):
                store(pr, rows, merged(pr, rows, new))
        return carry

    lax.fori_loop(0, d4 * blocks4 // DIL_UNROLL, body4, 0)

    def body1(nn, carry):
        specs = []
        for u in range(DIL_UNROLL):
            nb = DIL_UNROLL * nn + u
            q0 = pl.multiple_of(nb * DIL_BLOCK, DIL_BLOCK)
            specs.append((pl.ds(q0, DIL_BLOCK),
                          pl.ds(pl.multiple_of(ch + q0 - DIL_BLOCK, DIL_BLOCK), 2 * DIL_BLOCK),
                          jnp.logical_and(first_chunk, nb == 0)))
        for (rows, _, _), res in zip(specs, blocks(specs)):
            for pr, new in enumerate(res):
                n, d, _ = merged(pr, rows, new)
                o_ref[rows, pr * LANES:(pr + 1) * LANES] = (n / d).astype(o_ref.dtype)
        return carry

    lax.fori_loop(0, ch // DIL_BLOCK // DIL_UNROLL, body1, 0)

    kk_ref[:, 0:ch, :] = kk_ref[:, ch:2 * ch, :]
    vv_ref[:, 0:ch, :] = vv_ref[:, ch:2 * ch, :]


def _dil_attn(q, k, v):
    B, S, W = q.shape
    ch = DIL_CH
    blk = pl.BlockSpec((None, ch, W), lambda b, c: (b, c, 0))
    return pl.pallas_call(
        _dil_attn_kernel,
        grid=(B, S // ch),
        in_specs=[blk, blk, blk],
        out_specs=blk,
        out_shape=jax.ShapeDtypeStruct((B, S, W), BF16),
        scratch_shapes=[pltpu.VMEM((N_PAIRS, ch, LANES), F32), pltpu.VMEM((N_PAIRS, 2 * ch, LANES), F32),
                        pltpu.VMEM((N_PAIRS, 2 * ch, LANES), F32), pltpu.VMEM((N_PAIRS, ch, LANES), F32),
                        pltpu.VMEM((N_PAIRS, ch, LANES), F32), pltpu.VMEM((N_PAIRS, ch, LANES), F32),
                        pltpu.VMEM((2, 2 * DIL_BLOCK, 2 * DIL_BLOCK), F32)],
        compiler_params=_cparams(("arbitrary", "arbitrary")),
        name="dil_attn",
    )(q, k, v)


def _shift_rows(x, tail8, s):
    n = x.shape[0]
    row = lax.broadcasted_iota(jnp.int32, x.shape, 0)
    head = pltpu.repeat(pltpu.roll(tail8, s, 0), n // SUBLANES, axis=0)
    return jnp.where(row < s, head, pltpu.roll(x, s, 0))


def _expand_heads(v, n_heads, width):
    rows = v.shape[0]
    lane = lax.broadcasted_iota(jnp.int32, (rows, n_heads * width), 1)
    out = jnp.zeros((rows, n_heads * width), F32)
    for h in range(n_heads):
        out = jnp.where(lane // width == h, v[:, h:h + 1], out)
    return out


def _ssd_kernel(z_ref, xbc_ref, dt_ref, cw_ref, cb_ref, dtb_ref, alog_ref, dvec_ref, gain_ref, tri_ref,
                o_ref, state_ref, tail_ref):
    c = pl.program_id(1)
    lc = xbc_ref.shape[0]
    L = SSD_CHUNK

    @pl.when(c == 0)
    def _():
        state_ref[...] = jnp.zeros(state_ref.shape, F32)
        tail_ref[...] = jnp.zeros(tail_ref.shape, F32)

    xbc = xbc_ref[...]
    tail8 = tail_ref[...]
    cw = cw_ref[...]
    acc = xbc * cw[SSD_CONV - 1:SSD_CONV, :] + cb_ref[...]
    for kk in range(SSD_CONV - 1):
        acc = acc + _shift_rows(xbc, tail8, SSD_CONV - 1 - kk) * cw[kk:kk + 1, :]
    tail_ref[...] = xbc[lc - SUBLANES:lc, :]
    xbc_c = _silu(acc)

    xs_all = xbc_c[:, 0:SSD_INNER]
    dt_raw = dt_ref[...] + dtb_ref[...]
    dt_all = jnp.maximum(dt_raw, 0.0) + jnp.log(1.0 + jnp.exp(-jnp.abs(dt_raw)))
    a_all = dt_all * (-jnp.exp(alog_ref[...]))
    tri = tri_ref[...]
    lane_in = lax.broadcasted_iota(jnp.int32, (L, SSD_INNER), 1)
    grp_w = SSD_INNER // SSD_GROUPS
    ii = lax.broadcasted_iota(jnp.int32, (L, L), 0)
    jj = lax.broadcasted_iota(jnp.int32, (L, L), 1)
    causal = jj <= ii

    for ck in range(lc // L):
        rs = slice(ck * L, (ck + 1) * L)
        xs = xs_all[rs, :]
        a = a_all[rs, :]
        hi, mid, lo = _split3(a)
        acs = (jnp.dot(tri, hi, preferred_element_type=F32) + jnp.dot(tri, mid, preferred_element_type=F32)
               + jnp.dot(tri, lo, preferred_element_type=F32))
        acs_t = acs.T
        dt_e = _expand_heads(dt_all[rs, :], SSD_HEADS, SSD_P)
        acs_e = _expand_heads(acs, SSD_HEADS, SSD_P)
        last_e = acs_e[L - 1:L, :]
        xdt = xs * dt_e
        xdt_b = xdt.astype(BF16)
        xdtd_b = (xdt * jnp.exp(last_e - acs_e)).astype(BF16)
        prev = state_ref[...]
        prev_b = prev.astype(BF16)
        y = xs * dvec_ref[...]
        new_state = prev * jnp.exp(last_e)
        y_off = jnp.zeros((L, SSD_INNER), F32)
        for g in range(SSD_GROUPS):
            bg = xbc_c[rs, SSD_INNER + g * SSD_N:SSD_INNER + (g + 1) * SSD_N]
            cg = xbc_c[rs, SSD_INNER + (SSD_GROUPS + g) * SSD_N:SSD_INNER + (SSD_GROUPS + g + 1) * SSD_N]
            bg_b = bg.astype(BF16)
            cg_b = cg.astype(BF16)
            in_grp = lane_in // grp_w == g
            cb = lax.dot_general(cg_b, bg_b, (((1,), (1,)), ((), ())), preferred_element_type=F32)
            for hh in range(SSD_HEADS // SSD_GROUPS):
                hd = g * (SSD_HEADS // SSD_GROUPS) + hh
                seg = acs[:, hd:hd + 1] - acs_t[hd:hd + 1, :]
                w = jnp.where(causal, cb * jnp.exp(seg), 0.0).astype(BF16)
                y = y + jnp.dot(w, jnp.where(lane_in // SSD_P == hd, xdt_b, jnp.zeros_like(xdt_b)),
                                preferred_element_type=F32)
            zero_b = jnp.zeros_like(xdtd_b)
            new_state = new_state + jnp.dot(bg.T.astype(BF16), jnp.where(in_grp, xdtd_b, zero_b),
                                            preferred_element_type=F32)
            y_off = y_off + jnp.dot(cg_b, jnp.where(in_grp, prev_b, jnp.zeros_like(prev_b)),
                                    preferred_element_type=F32)
        state_ref[...] = new_state
        y = y + y_off * jnp.exp(acs_e)
        y = y * _silu(z_ref[rs, :].astype(F32))
        ysq = y * y
        in0 = lane_in < grp_w
        ms0 = jnp.sum(jnp.where(in0, ysq, 0.0), axis=1, keepdims=True) * (1.0 / grp_w)
        ms1 = jnp.sum(jnp.where(in0, 0.0, ysq), axis=1, keepdims=True) * (1.0 / grp_w)
        r = jnp.where(in0, lax.rsqrt(ms0 + EPS), lax.rsqrt(ms1 + EPS))
        o_ref[rs, :] = ((y * r) * gain_ref[...]).astype(o_ref.dtype)


def _ssd(z, xbc, dt, cw, cb, dtb, alog, dvec, gain, tri):
    B, S, _ = z.shape
    lc = SSD_LC
    tok = lambda w: pl.BlockSpec((None, lc, w), lambda b, c: (b, c, 0))
    consts = (cw, cb, dtb, alog, dvec, gain, tri)
    return pl.pallas_call(
        _ssd_kernel,
        grid=(B, S // lc),
        in_specs=[tok(SSD_INNER), tok(SSD_XBC), tok(LANES)] + [_const_spec(a.shape) for a in consts],
        out_specs=tok(SSD_INNER),
        out_shape=jax.ShapeDtypeStruct((B, S, SSD_INNER), BF16),
        scratch_shapes=[pltpu.VMEM((SSD_N, SSD_INNER), F32), pltpu.VMEM((SUBLANES, SSD_XBC), F32)],
        compiler_params=_cparams(("arbitrary", "arbitrary")),
        name="ssd",
    )(z, xbc, dt, *consts)


def _rms(x, gain):
    ms = jnp.mean(x * x, axis=-1, keepdims=True)
    return (x * lax.rsqrt(ms + EPS)) * gain


def _out_ffn_kernel(x_ref, od_ref, ol_ref, os_ref, wo_d_ref, wo_l_ref, wo_s_ref, g_mix_ref, g_pre_ref,
                    up_ref, cw_ref, cb_ref, down_ref, g_post_ref, o_ref, tail_ref, *, tiles_per_seq):
    t = pl.program_id(0)
    tm = x_ref.shape[0]

    @pl.when(t % tiles_per_seq == 0)
    def _():
        tail_ref[...] = jnp.zeros(tail_ref.shape, F32)

    mix = (jnp.dot(od_ref[...], wo_d_ref[...], preferred_element_type=F32)
           + jnp.dot(ol_ref[...], wo_l_ref[...], preferred_element_type=F32)
           + jnp.dot(os_ref[...], wo_s_ref[...], preferred_element_type=F32))
    x1 = x_ref[...] + _rms(mix, g_mix_ref[...])
    h = _rms(x1, g_pre_ref[...]).astype(BF16)

    def up(ci):
        return [jnp.dot(h, up_ref[:, off + ci * FF_CHUNK:off + (ci + 1) * FF_CHUNK], preferred_element_type=F32)
                for off in (0, D_FF)]

    n_chunks = D_FF // FF_CHUNK
    f_acc = jnp.zeros((tm, D_MODEL), F32)
    pending = [up(ci) for ci in range(FF_AHEAD)]
    for ci in range(n_chunks):
        u_pair = pending.pop(0)
        if ci + FF_AHEAD < n_chunks:
            pending.append(up(ci + FF_AHEAD))
        halves = []
        for u, off in zip(u_pair, (0, D_FF)):
            cs = slice(off + ci * FF_CHUNK, off + (ci + 1) * FF_CHUNK)
            tail8 = tail_ref[:, cs]
            cw = cw_ref[:, cs]
            conv = u * cw[FFN_CONV - 1:FFN_CONV, :] + cb_ref[:, cs]
            for kk in range(FFN_CONV - 1):
                conv = conv + _shift_rows(u, tail8, FFN_CONV - 1 - kk) * cw[kk:kk + 1, :]
            tail_ref[:, cs] = u[tm - SUBLANES:tm, :]
            halves.append(conv)
        f = (_silu(halves[0]) * halves[1]).astype(BF16)
        f_acc = f_acc + jnp.dot(f, down_ref[ci * FF_CHUNK:(ci + 1) * FF_CHUNK, :], preferred_element_type=F32)
    o_ref[...] = x1 + _rms(f_acc, g_post_ref[...])


def _out_ffn(x2d, od, ol, os_, wo_d, wo_l, wo_s, g_mix, g_pre, up, cw, cb, down, g_post, tiles_per_seq):
    T, D = x2d.shape
    tm = TM_FFN
    tok = lambda w: pl.BlockSpec((tm, w), lambda t: (t, 0))
    consts = (wo_d, wo_l, wo_s, g_mix, g_pre, up, cw, cb, down, g_post)
    return pl.pallas_call(
        functools.partial(_out_ffn_kernel, tiles_per_seq=tiles_per_seq),
        grid=(T // tm,),
        in_specs=[tok(D), tok(DIFF_W), tok(DIL_W), tok(SSD_INNER)]
                 + [pl.BlockSpec(a.shape, lambda t: (0, 0), pipeline_mode=pl.Buffered(1)) for a in consts],
        out_specs=tok(D),
        out_shape=jax.ShapeDtypeStruct((T, D), F32),
        scratch_shapes=[pltpu.VMEM((SUBLANES, 2 * D_FF), F32)],
        compiler_params=_cparams(("arbitrary",)),
        name="out_ffn",
    )(x2d, od, ol, os_, *consts)


def _rope_tables(S, half):
    inv_freq = jnp.exp(-math.log(ROPE_THETA) * jnp.arange(half, dtype=F32) / half)
    ang = jnp.arange(S, dtype=F32)[:, None] * inv_freq[None, :]
    cos, sin = jnp.cos(ang), jnp.sin(ang)
    zero = jnp.zeros_like(sin)
    reps = LANES // (2 * half)
    cos_t = jnp.tile(jnp.concatenate([cos, cos], axis=1), (1, reps))
    sin_up = jnp.tile(jnp.concatenate([-sin, zero], axis=1), (1, reps))
    sin_dn = jnp.tile(jnp.concatenate([zero, sin], axis=1), (1, reps))
    return cos_t, sin_up, sin_dn


def _rope_tables_t(S, half):
    inv_freq = jnp.exp(-math.log(ROPE_THETA) * jnp.arange(half, dtype=F32) / half)
    ang = jnp.arange(S, dtype=F32)[:, None] * inv_freq[None, :]
    return jnp.cos(ang).T, jnp.sin(ang).T


def kernel(x, pre_mix_norm, w_in, diff_lambda, diff_head_norm, ssd_conv_w, ssd_conv_b, ssd_dt_bias, ssd_A_log,
           ssd_D, ssd_norm, w_out, post_mix_norm, pre_ffn_norm, ffn_up, ffn_conv_w, ffn_conv_b, ffn_down,
           post_ffn_norm):
    B, S, D = x.shape
    depth = w_in.shape[0]
    assert D == D_MODEL and S % DIL_CH == 0 and S % TM_IN == 0 and S % BQ == 0 and S % TM_FFN == 0
    tables = _rope_tables_t(S, DIFF_QK // 2) + _rope_tables(S, DIFF_QK // 2) + _rope_tables(S, DIL_DH // 2)
    tri = (lax.broadcasted_iota(jnp.int32, (SSD_CHUNK, SSD_CHUNK), 1)
           <= lax.broadcasted_iota(jnp.int32, (SSD_CHUNK, SSD_CHUNK), 0)).astype(BF16)
    offs = [0]
    for sz in IN_SIZES:
        offs.append(offs[-1] + sz)

    def pad_lanes(v):
        return jnp.pad(v, ((0, 0), (0, LANES - v.shape[1])))

    for layer in range(depth):
        lambda_init = 0.8 - 0.6 * math.exp(-0.3 * layer)
        wl = w_in[layer].astype(BF16)
        w_a = jnp.concatenate([wl[:, offs[1]:offs[2]], wl[:, offs[3]:offs[5]]], axis=1)
        w_b = jnp.pad(wl[:, offs[5]:], ((0, 0), (0, LANES - SSD_HEADS)))
        ws = [wl[:, offs[0]:offs[1]].T, w_a, w_b, wl[:, offs[2]:offs[3]].T]
        dqt, dk, dvt, lq, lk, lv, z, xbc, dt = _in_proj(x, pre_mix_norm[layer][None, :], tables, ws)

        o_diff = _diff_attn(dqt, dk, dvt, diff_lambda[layer],
                            jnp.broadcast_to(diff_head_norm[layer][:, None], (DIFF_V, LANES)), lambda_init)
        o_dil = _dil_attn(lq, lk, lv)
        o_ssd = _ssd(z, xbc, dt, ssd_conv_w[layer], ssd_conv_b[layer][None, :],
                     pad_lanes(ssd_dt_bias[layer][None, :]), pad_lanes(ssd_A_log[layer][None, :]),
                     jnp.repeat(ssd_D[layer], SSD_P)[None, :], ssd_norm[layer][None, :], tri)

        wo = w_out[layer].astype(BF16)
        x2d = _out_ffn(
            x.reshape(B * S, D), o_diff.reshape(B * S, DIFF_W), o_dil.reshape(B * S, DIL_W),
            o_ssd.reshape(B * S, SSD_INNER),
            wo[0:DIFF_W], wo[DIFF_W:DIFF_W + DIL_W], wo[DIFF_W + DIL_W:],
            post_mix_norm[layer][None, :], pre_ffn_norm[layer][None, :],
            ffn_up[layer].astype(BF16), ffn_conv_w[layer], ffn_conv_b[layer][None, :],
            ffn_down[layer].astype(BF16), post_ffn_norm[layer][None, :], S // TM_FFN)
        x = x2d.reshape(B, S, D)
    return x
```

```python
import functools
import math

import jax
import jax.numpy as jnp
from jax import lax
from jax.experimental import pallas as pl
from jax.experimental.pallas import tpu as pltpu

F32 = jnp.float32
BF16 = jnp.bfloat16

LANES = 128
SUBLANES = 8
VMEM_LIMIT = 56 * 1024 * 1024

D_MODEL = 1024
DIFF_HEADS = 4
DIFF_QK = 32
DIFF_V = 64
DIFF_W = DIFF_HEADS * DIFF_V
DIL_HEADS = 6
DIL_DH = 64
DIL_W = DIL_HEADS * DIL_DH
DIL_PATTERNS = ((128, 1), (512, 4), (2048, 16))
DIL_BLOCK = 128
SSD_HEADS = 6
SSD_P = 64
SSD_GROUPS = 2
SSD_N = 128
SSD_CONV = 4
SSD_CHUNK = 128
SSD_INNER = SSD_HEADS * SSD_P
SSD_XBC = SSD_INNER + 2 * SSD_GROUPS * SSD_N
D_FF = 2816
FFN_CONV = 3
ROPE_THETA = 10000.0
EPS = 1e-6
LOG2E = 1.4426950408889634

IN_SIZES = (DIFF_W, DIFF_W, DIFF_W, DIL_W, DIL_W, DIL_W, SSD_INNER, SSD_XBC, SSD_HEADS)

TM_IN = 512
BQ = 256
BK = 2 * BQ
DIL_CH = 2048
SSD_LC = 512
TM_FFN = 256
FF_CHUNK = 256
FF_AHEAD = 2


def _cparams(sem):
    return pltpu.CompilerParams(dimension_semantics=sem, vmem_limit_bytes=VMEM_LIMIT)


def _const_spec(shape):
    nd = len(shape)
    return pl.BlockSpec(shape, lambda *_: (0,) * nd)


def _silu(x):
    return x * (1.0 / (1.0 + jnp.exp(-x)))


def _split3(x):
    hi = x.astype(BF16)
    r = x - hi.astype(F32)
    mid = r.astype(BF16)
    lo = (r - mid.astype(F32)).astype(BF16)
    return hi, mid, lo


def _rope_lanes(y, cos, sin_up, sin_dn, half):
    outs = []
    for j in range(y.shape[1] // LANES):
        yc = y[:, j * LANES:(j + 1) * LANES]
        up = pltpu.roll(yc, LANES - half, 1)
        dn = pltpu.roll(yc, half, 1)
        outs.append(yc * cos + up * sin_up + dn * sin_dn)
    return jnp.concatenate(outs, axis=1)


def _rope_rows(yt, cos_t, sin_t, half):
    outs = []
    for g in range(yt.shape[0] // (2 * half)):
        x1 = yt[2 * half * g:2 * half * g + half, :]
        x2 = yt[2 * half * g + half:2 * half * (g + 1), :]
        outs += [x1 * cos_t - x2 * sin_t, x2 * cos_t + x1 * sin_t]
    return jnp.concatenate(outs, axis=0)


def _in_proj_kernel(x_ref, g_ref, cdt_ref, sdt_ref, cd_ref, sud_ref, sdd_ref, cl_ref, sul_ref, sdl_ref,
                    wqt_ref, wa_ref, wb_ref, wvt_ref,
                    oqt_ref, ok_ref, ovt_ref, olq_ref, olk_ref, olv_ref, oz_ref, oxbc_ref, odt_ref):
    x = x_ref[...]
    ms = jnp.mean(x * x, axis=-1, keepdims=True)
    h = ((x * lax.rsqrt(ms + EPS)) * g_ref[...]).astype(BF16)

    def proj(w_ref):
        return jnp.dot(h, w_ref[...], preferred_element_type=F32)

    def proj_t(wt_ref):
        return lax.dot_general(wt_ref[...], h, (((1,), (1,)), ((), ())), preferred_element_type=F32)

    r_qt = proj_t(wqt_ref)
    r_a = proj(wa_ref)
    r_b = proj(wb_ref)
    r_vt = proj_t(wvt_ref)

    qt = _rope_rows(r_qt, cdt_ref[...], sdt_ref[...], DIFF_QK // 2)
    oqt_ref[...] = (qt * (DIFF_QK ** -0.5 * LOG2E)).astype(BF16)
    cd, sud, sdd = cd_ref[...], sud_ref[...], sdd_ref[...]
    a0, a1, a2 = DIFF_W, DIFF_W + DIL_W, DIFF_W + 2 * DIL_W
    ok_ref[...] = _rope_lanes(r_a[:, 0:a0], cd, sud, sdd, DIFF_QK // 2).astype(BF16)
    cl, sul, sdl = cl_ref[...], sul_ref[...], sdl_ref[...]
    lq = _rope_lanes(r_a[:, a0:a1], cl, sul, sdl, DIL_DH // 2)
    olq_ref[...] = (lq * (DIL_DH ** -0.5)).astype(BF16)
    olk_ref[...] = _rope_lanes(r_a[:, a1:a2], cl, sul, sdl, DIL_DH // 2).astype(BF16)
    b0, b1, b2 = DIL_W, DIL_W + SSD_INNER, DIL_W + SSD_INNER + SSD_XBC
    olv_ref[...] = r_b[:, 0:b0].astype(BF16)
    oz_ref[...] = r_b[:, b0:b1].astype(BF16)
    oxbc_ref[...] = r_b[:, b1:b2]
    odt_ref[...] = r_b[:, b2:b2 + LANES]
    ovt_ref[...] = r_vt.astype(BF16)


def _in_proj(x, gain, tables, ws):
    B, S, D = x.shape
    tm = TM_IN
    grid = (S // tm, B)
    tok = lambda w: pl.BlockSpec((None, tm, w), lambda s, b: (b, s, 0))
    tok_t = lambda w: pl.BlockSpec((None, w, tm), lambda s, b: (b, 0, s))
    tab = pl.BlockSpec((tm, LANES), lambda s, b: (s, 0))
    tab_t = pl.BlockSpec((DIFF_QK // 2, tm), lambda s, b: (0, s))
    out_widths = (DIFF_W, DIFF_W, DIFF_W, DIL_W, DIL_W, DIL_W, SSD_INNER, SSD_XBC, LANES)
    out_dtypes = (BF16, BF16, BF16, BF16, BF16, BF16, BF16, F32, F32)
    transposed = (True, False, True) + (False,) * 6
    return pl.pallas_call(
        _in_proj_kernel,
        grid=grid,
        in_specs=[tok(D), _const_spec((1, D))] + [tab_t] * 2 + [tab] * 6 + [_const_spec(w.shape) for w in ws],
        out_specs=[tok_t(w) if t else tok(w) for w, t in zip(out_widths, transposed)],
        out_shape=[jax.ShapeDtypeStruct((B, w, S) if t else (B, S, w), dt)
                   for w, dt, t in zip(out_widths, out_dtypes, transposed)],
        compiler_params=_cparams(("arbitrary", "arbitrary")),
        name="in_proj",
    )(x, gain, *tables, *ws)


N_SM = 2 * DIFF_HEADS
ACC_ROWS = DIFF_V + 16
AHEAD = 0


def _diff_attn_kernel(lam_ref, gain_ref, qt_ref, k_ref, vt_ref, o_ref, qs_ref, m_ref, acc_ref, sa_ref, sb_ref, *,
                      lambda_init):
    i = pl.program_id(1)
    bq = qt_ref.shape[1]
    qt = qt_ref[...]
    feat = lax.broadcasted_iota(jnp.int32, (DIFF_W, bq), 0)
    zero = jnp.zeros_like(qt)
    for g in range(N_SM):
        qs_ref[:, g * bq:(g + 1) * bq] = jnp.where(feat // DIFF_QK == g, qt, zero)
    m_ref[...] = jnp.full(m_ref.shape, -jnp.inf, F32)
    acc_ref[...] = jnp.zeros(acc_ref.shape, F32)
    bk = sa_ref.shape[0]
    ones = jnp.ones((ACC_ROWS - DIFF_V, bk), BF16)

    def key_tile(j):
        return k_ref[pl.ds(pl.multiple_of(j * bk, bk), bk), :]

    def scores_into(dst_ref, kj, g):
        cols = slice(g * bq, (g + 1) * bq)
        dst_ref[:, cols] = jnp.dot(kj, qs_ref[:, cols], preferred_element_type=F32)

    def step(src_ref, j, diag, dst_ref):
        vtj = vt_ref[:, pl.ds(pl.multiple_of(j * bk, bk), bk)]
        masked = diag is not None
        if masked:
            key = lax.broadcasted_iota(jnp.int32, (bk, bq), 0)
            qry = lax.broadcasted_iota(jnp.int32, (bk, bq), 1) + diag
            visible = key <= qry
        if dst_ref is not None:
            kn = key_tile(j + 1)
            for g in range(AHEAD):
                scores_into(dst_ref, kn, g)
        m_all = m_ref[...]
        new_m, new_acc = [], []
        for g in range(N_SM):
            s = src_ref[:, g * bq:(g + 1) * bq]
            if masked:
                s = jnp.where(visible, s, -jnp.inf)
            m_prev = m_all[:, g * bq:(g + 1) * bq]
            m_new = jnp.maximum(m_prev, jnp.max(s, axis=0, keepdims=True))
            alpha = jnp.exp2(m_prev - m_new)
            p = jnp.exp2(s - m_new).astype(BF16)
            if dst_ref is not None and g + AHEAD < N_SM:
                scores_into(dst_ref, kn, g + AHEAD)
            hd = g // 2
            lhs = jnp.concatenate([vtj[hd * DIFF_V:(hd + 1) * DIFF_V, :], ones], axis=0)
            pv = jnp.dot(lhs, p, preferred_element_type=F32)
            new_m.append(m_new)
            new_acc.append(acc_ref[g] * alpha + pv)
        m_ref[...] = jnp.concatenate(new_m, axis=1)
        for g in range(N_SM):
            acc_ref[g] = new_acc[g]

    k0 = key_tile(0)
    for g in range(N_SM):
        scores_into(sa_ref, k0, g)

    n = i // 2
    q_off = (i % 2) * bq

    def body(jj, carry):
        step(sa_ref, 2 * jj, None, sb_ref)
        step(sb_ref, 2 * jj + 1, None, sa_ref)
        return carry

    lax.fori_loop(0, n // 2, body, 0)

    @pl.when(n % 2 == 0)
    def _():
        step(sa_ref, n, q_off, None)

    @pl.when(n % 2 == 1)
    def _():
        step(sa_ref, n - 1, None, sb_ref)
        step(sb_ref, n, q_off, None)

    lam_p = lam_ref[...]
    s1 = jnp.sum(lam_p[0:1, :] * lam_p[1:2, :], axis=1, keepdims=True)
    s2 = jnp.sum(lam_p[2:3, :] * lam_p[3:4, :], axis=1, keepdims=True)
    lam = jnp.exp(s1) - jnp.exp(s2) + lambda_init
    gain = jnp.tile(gain_ref[...], (1, bq // LANES))
    outs = []
    for hd in range(DIFF_HEADS):
        a1 = acc_ref[2 * hd]
        a2 = acc_ref[2 * hd + 1]
        o1 = a1[0:DIFF_V, :] / a1[DIFF_V:DIFF_V + 1, :]
        o2 = a2[0:DIFF_V, :] / a2[DIFF_V:DIFF_V + 1, :]
        d = o1 - lam * o2
        ms = jnp.mean(d * d, axis=0, keepdims=True)
        outs.append(((d * lax.rsqrt(ms + EPS)) * gain) * (1.0 - lambda_init))
    o_ref[...] = jnp.concatenate(outs, axis=0).T.astype(o_ref.dtype)


def _diff_attn(qt, k, vt, lam_p, gain_b, lambda_init):
    B, S, W = k.shape
    grid = (B, S // BQ)
    return pl.pallas_call(
        functools.partial(_diff_attn_kernel, lambda_init=lambda_init),
        grid=grid,
        in_specs=[_const_spec(lam_p.shape), _const_spec(gain_b.shape),
                  pl.BlockSpec((None, W, BQ), lambda b, i: (b, 0, i)),
                  pl.BlockSpec((None, S, W), lambda b, i: (b, 0, 0)),
                  pl.BlockSpec((None, W, S), lambda b, i: (b, 0, 0))],
        out_specs=pl.BlockSpec((None, BQ, W), lambda b, i: (b, i, 0)),
        out_shape=jax.ShapeDtypeStruct((B, S, W), BF16),
        scratch_shapes=[pltpu.VMEM((W, N_SM * BQ), BF16),
                        pltpu.VMEM((1, N_SM * BQ), F32),
                        pltpu.VMEM((N_SM, ACC_ROWS, BQ), F32),
                        pltpu.VMEM((BK, N_SM * BQ), F32),
                        pltpu.VMEM((BK, N_SM * BQ), F32)],
        compiler_params=_cparams(("arbitrary", "arbitrary")),
        name="diff_attn",
    )(lam_p, gain_b, qt, k, vt)


N_PAIRS = DIL_W // LANES
DIL_UNROLL = 2


def _dil_block(qs, ks, vs, biases):
    nq = DIL_BLOCK
    low = lax.broadcasted_iota(jnp.int32, (nq, LANES), 1) < DIL_DH
    scores = []
    for qp, kp, bias in zip(qs, ks, biases):
        q2 = jnp.concatenate([jnp.where(low, qp, 0.0), jnp.where(low, 0.0, qp)], axis=0).astype(BF16)
        scores.append(lax.dot_general(q2, kp.astype(BF16), (((1,), (1,)), ((), ())),
                                      preferred_element_type=F32) + bias)
    out = []
    for s, vp in zip(scores, vs):
        m = jnp.max(s, axis=1, keepdims=True)
        p = jnp.exp(s - m)
        den = jnp.sum(p, axis=1, keepdims=True)
        num = jnp.dot(p.astype(BF16), vp.astype(BF16), preferred_element_type=F32)
        out.append(tuple(jnp.where(low, t[0:nq], t[nq:2 * nq]) for t in (num, den, m)))
    return out


def _dil_merge(a, b):
    (n1, d1, m1), (n2, d2, m2) = a, b
    m = jnp.maximum(m1, m2)
    w1 = jnp.exp(m1 - m)
    w2 = jnp.exp(m2 - m)
    return n1 * w1 + n2 * w2, d1 * w1 + d2 * w2, m


def _dil_attn_kernel(q_ref, k_ref, v_ref, o_ref, qf_ref, kk_ref, vv_ref, num_ref, den_ref, max_ref, bias_ref):
    c = pl.program_id(1)
    ch = q_ref.shape[0]

    iq = lax.broadcasted_iota(jnp.int32, (2 * DIL_BLOCK, 2 * DIL_BLOCK), 0) % DIL_BLOCK
    jk = lax.broadcasted_iota(jnp.int32, (2 * DIL_BLOCK, 2 * DIL_BLOCK), 1)
    band = (jk >= iq) & (jk <= iq + DIL_BLOCK)
    bias_ref[0] = jnp.where(band, 0.0, -jnp.inf)
    bias_ref[1] = jnp.where(band & (jk >= DIL_BLOCK), 0.0, -jnp.inf)

    @pl.when(c == 0)
    def _():
        kk_ref[:, 0:ch, :] = jnp.zeros((N_PAIRS, ch, LANES), F32)
        vv_ref[:, 0:ch, :] = jnp.zeros((N_PAIRS, ch, LANES), F32)

    for pr in range(N_PAIRS):
        sl = slice(pr * LANES, (pr + 1) * LANES)
        qf_ref[pr] = q_ref[:, sl].astype(F32)
        kk_ref[pr, ch:2 * ch, :] = k_ref[:, sl].astype(F32)
        vv_ref[pr, ch:2 * ch, :] = v_ref[:, sl].astype(F32)
    first_chunk = c == 0

    def blocks(specs):
        qs, ks, vs, biases = [], [], [], []
        for rows, keys, first_block in specs:
            bias = bias_ref[first_block.astype(jnp.int32)]
            for pr in range(N_PAIRS):
                qs.append(qf_ref[pr, rows, :])
                ks.append(kk_ref[pr, keys, :])
                vs.append(vv_ref[pr, keys, :])
                biases.append(bias)
        res = _dil_block(qs, ks, vs, biases)
        return [res[t * N_PAIRS:(t + 1) * N_PAIRS] for t in range(len(specs))]

    def merged(pr, rows, new):
        return _dil_merge((num_ref[pr, rows, :], den_ref[pr, rows, :], max_ref[pr, rows, :]), new)

    def store(pr, rows, ndm):
        num_ref[pr, rows, :] = ndm[0]
        den_ref[pr, rows, :] = ndm[1]
        max_ref[pr, rows, :] = ndm[2]

    d16 = DIL_PATTERNS[2][1]

    def body16(rr, carry):
        specs = [(pl.ds(DIL_UNROLL * rr + u, DIL_BLOCK, stride=d16),
                  pl.ds(DIL_UNROLL * rr + u, 2 * DIL_BLOCK, stride=d16), first_chunk) for u in range(DIL_UNROLL)]
        for (rows, _, _), res in zip(specs, blocks(specs)):
            for pr, new in enumerate(res):
                store(pr, rows, new)
        return carry

    lax.fori_loop(0, d16 // DIL_UNROLL, body16, 0)

    d4 = DIL_PATTERNS[1][1]
    blocks4 = ch // (d4 * DIL_BLOCK)

    def body4(tt, carry):
        nb = (DIL_UNROLL * tt) // d4
        first_block = jnp.logical_and(first_chunk, nb == 0)
        specs = []
        for u in range(DIL_UNROLL):
            q0 = (DIL_UNROLL * tt) % d4 + u + nb * (d4 * DIL_BLOCK)
            specs.append((pl.ds(q0, DIL_BLOCK, stride=d4),
                          pl.ds(ch + q0 - d4 * DIL_BLOCK, 2 * DIL_BLOCK, stride=d4), first_block))
        for (rows, _, _), res in zip(specs, blocks(specs)):
            for pr, new in enumerate(res):
                store(pr, rows, merged(pr, rows, new))
        return carry

    lax.fori_loop(0, d4 * blocks4 // DIL_UNROLL, body4, 0)

    def body1(nn, carry):
        specs = []
        for u in range(DIL_UNROLL):
            nb = DIL_UNROLL * nn + u
            q0 = pl.multiple_of(nb * DIL_BLOCK, DIL_BLOCK)
            specs.append((pl.ds(q0, DIL_BLOCK),
                          pl.ds(pl.multiple_of(ch + q0 - DIL_BLOCK, DIL_BLOCK), 2 * DIL_BLOCK),
                          jnp.logical_and(first_chunk, nb == 0)))
        for (rows, _, _), res in zip(specs, blocks(specs)):
            for pr, new in enumerate(res):
                n, d, _ = merged(pr, rows, new)
                o_ref[rows, pr * LANES:(pr + 1) * LANES] = (n / d).astype(o_ref.dtype)
        return carry

    lax.fori_loop(0, ch // DIL_BLOCK // DIL_UNROLL, body1, 0)

    kk_ref[:, 0:ch, :] = kk_ref[:, ch:2 * ch, :]
    vv_ref[:, 0:ch, :] = vv_ref[:, ch:2 * ch, :]


def _dil_attn(q, k, v):
    B, S, W = q.shape
    ch = DIL_CH
    blk = pl.BlockSpec((None, ch, W), lambda b, c: (b, c, 0))
    return pl.pallas_call(
        _dil_attn_kernel,
        grid=(B, S // ch),
        in_specs=[blk, blk, blk],
        out_specs=blk,
        out_shape=jax.ShapeDtypeStruct((B, S, W), BF16),
        scratch_shapes=[pltpu.VMEM((N_PAIRS, ch, LANES), F32), pltpu.VMEM((N_PAIRS, 2 * ch, LANES), F32),
                        pltpu.VMEM((N_PAIRS, 2 * ch, LANES), F32), pltpu.VMEM((N_PAIRS, ch, LANES), F32),
                        pltpu.VMEM((N_PAIRS, ch, LANES), F32), pltpu.VMEM((N_PAIRS, ch, LANES), F32),
                        pltpu.VMEM((2, 2 * DIL_BLOCK, 2 * DIL_BLOCK), F32)],
        compiler_params=_cparams(("arbitrary", "arbitrary")),
        name="dil_attn",
    )(q, k, v)


def _shift_rows(x, tail8, s):
    n = x.shape[0]
    row = lax.broadcasted_iota(jnp.int32, x.shape, 0)
    head = jnp.tile(pltpu.roll(tail8, s, 0), (n // SUBLANES, 1))
    return jnp.where(row < s, head, pltpu.roll(x, s, 0))


def _expand_heads(v, n_heads, width):
    rows = v.shape[0]
    lane = lax.broadcasted_iota(jnp.int32, (rows, n_heads * width), 1)
    out = jnp.zeros((rows, n_heads * width), F32)
    for h in range(n_heads):
        out = jnp.where(lane // width == h, v[:, h:h + 1], out)
    return out


def _ssd_kernel(z_ref, xbc_ref, dt_ref, cw_ref, cb_ref, dtb_ref, alog_ref, dvec_ref, gain_ref, tri_ref,
                o_ref, state_ref, tail_ref):
    c = pl.program_id(1)
    lc = xbc_ref.shape[0]
    L = SSD_CHUNK

    @pl.when(c == 0)
    def _():
        state_ref[...] = jnp.zeros(state_ref.shape, F32)
        tail_ref[...] = jnp.zeros(tail_ref.shape, F32)

    xbc = xbc_ref[...]
    tail8 = tail_ref[...]
    cw = cw_ref[...]
    acc = xbc * cw[SSD_CONV - 1:SSD_CONV, :] + cb_ref[...]
    for kk in range(SSD_CONV - 1):
        acc = acc + _shift_rows(xbc, tail8, SSD_CONV - 1 - kk) * cw[kk:kk + 1, :]
    tail_ref[...] = xbc[lc - SUBLANES:lc, :]
    xbc_c = _silu(acc)

    xs_all = xbc_c[:, 0:SSD_INNER]
    dt_raw = dt_ref[...] + dtb_ref[...]
    dt_all = jnp.maximum(dt_raw, 0.0) + jnp.log(1.0 + jnp.exp(-jnp.abs(dt_raw)))
    a_all = dt_all * (-jnp.exp(alog_ref[...]))
    tri = tri_ref[...]
    lane_in = lax.broadcasted_iota(jnp.int32, (L, SSD_INNER), 1)
    grp_w = SSD_INNER // SSD_GROUPS
    ii = lax.broadcasted_iota(jnp.int32, (L, L), 0)
    jj = lax.broadcasted_iota(jnp.int32, (L, L), 1)
    causal = jj <= ii

    for ck in range(lc // L):
        rs = slice(ck * L, (ck + 1) * L)
        xs = xs_all[rs, :]
        a = a_all[rs, :]
        hi, mid, lo = _split3(a)
        acs = (jnp.dot(tri, hi, preferred_element_type=F32) + jnp.dot(tri, mid, preferred_element_type=F32)
               + jnp.dot(tri, lo, preferred_element_type=F32))
        acs_t = acs.T
        dt_e = _expand_heads(dt_all[rs, :], SSD_HEADS, SSD_P)
        acs_e = _expand_heads(acs, SSD_HEADS, SSD_P)
        last_e = acs_e[L - 1:L, :]
        xdt = xs * dt_e
        xdt_b = xdt.astype(BF16)
        xdtd_b = (xdt * jnp.exp(last_e - acs_e)).astype(BF16)
        prev = state_ref[...]
        prev_b = prev.astype(BF16)
        y = xs * dvec_ref[...]
        new_state = prev * jnp.exp(last_e)
        y_off = jnp.zeros((L, SSD_INNER), F32)
        for g in range(SSD_GROUPS):
            bg = xbc_c[rs, SSD_INNER + g * SSD_N:SSD_INNER + (g + 1) * SSD_N]
            cg = xbc_c[rs, SSD_INNER + (SSD_GROUPS + g) * SSD_N:SSD_INNER + (SSD_GROUPS + g + 1) * SSD_N]
            bg_b = bg.astype(BF16)
            cg_b = cg.astype(BF16)
            in_grp = lane_in // grp_w == g
            cb = lax.dot_general(cg_b, bg_b, (((1,), (1,)), ((), ())), preferred_element_type=F32)
            for hh in range(SSD_HEADS // SSD_GROUPS):
                hd = g * (SSD_HEADS // SSD_GROUPS) + hh
                seg = acs[:, hd:hd + 1] - acs_t[hd:hd + 1, :]
                w = jnp.where(causal, cb * jnp.exp(seg), 0.0).astype(BF16)
                y = y + jnp.dot(w, jnp.where(lane_in // SSD_P == hd, xdt_b, jnp.zeros_like(xdt_b)),
                                preferred_element_type=F32)
            zero_b = jnp.zeros_like(xdtd_b)
            new_state = new_state + jnp.dot(bg.T.astype(BF16), jnp.where(in_grp, xdtd_b, zero_b),
                                            preferred_element_type=F32)
            y_off = y_off + jnp.dot(cg_b, jnp.where(in_grp, prev_b, jnp.zeros_like(prev_b)),
                                    preferred_element_type=F32)
        state_ref[...] = new_state
        y = y + y_off * jnp.exp(acs_e)
        y = y * _silu(z_ref[rs, :].astype(F32))
        ysq = y * y
        in0 = lane_in < grp_w
        ms0 = jnp.sum(jnp.where(in0, ysq, 0.0), axis=1, keepdims=True) * (1.0 / grp_w)
        ms1 = jnp.sum(jnp.where(in0, 0.0, ysq), axis=1, keepdims=True) * (1.0 / grp_w)
        r = jnp.where(in0, lax.rsqrt(ms0 + EPS), lax.rsqrt(ms1 + EPS))
        o_ref[rs, :] = ((y * r) * gain_ref[...]).astype(o_ref.dtype)


def _ssd(z, xbc, dt, cw, cb, dtb, alog, dvec, gain, tri):
    B, S, _ = z.shape
    lc = SSD_LC
    tok = lambda w: pl.BlockSpec((None, lc, w), lambda b, c: (b, c, 0))
    consts = (cw, cb, dtb, alog, dvec, gain, tri)
    return pl.pallas_call(
        _ssd_kernel,
        grid=(B, S // lc),
        in_specs=[tok(SSD_INNER), tok(SSD_XBC), tok(LANES)] + [_const_spec(a.shape) for a in consts],
        out_specs=tok(SSD_INNER),
        out_shape=jax.ShapeDtypeStruct((B, S, SSD_INNER), BF16),
        scratch_shapes=[pltpu.VMEM((SSD_N, SSD_INNER), F32), pltpu.VMEM((SUBLANES, SSD_XBC), F32)],
        compiler_params=_cparams(("arbitrary", "arbitrary")),
        name="ssd",
    )(z, xbc, dt, *consts)


def _rms(x, gain):
    ms = jnp.mean(x * x, axis=-1, keepdims=True)
    return (x * lax.rsqrt(ms + EPS)) * gain


def _out_ffn_kernel(x_ref, od_ref, ol_ref, os_ref, wo_d_ref, wo_l_ref, wo_s_ref, g_mix_ref, g_pre_ref,
                    up_ref, cw_ref, cb_ref, down_ref, g_post_ref, o_ref, tail_ref, *, tiles_per_seq):
    t = pl.program_id(0)
    tm = x_ref.shape[0]

    @pl.when(t % tiles_per_seq == 0)
    def _():
        tail_ref[...] = jnp.zeros(tail_ref.shape, F32)

    mix = (jnp.dot(od_ref[...], wo_d_ref[...], preferred_element_type=F32)
           + jnp.dot(ol_ref[...], wo_l_ref[...], preferred_element_type=F32)
           + jnp.dot(os_ref[...], wo_s_ref[...], preferred_element_type=F32))
    x1 = x_ref[...] + _rms(mix, g_mix_ref[...])
    h = _rms(x1, g_pre_ref[...]).astype(BF16)

    def up(ci):
        return [jnp.dot(h, up_ref[:, off + ci * FF_CHUNK:off + (ci + 1) * FF_CHUNK], preferred_element_type=F32)
                for off in (0, D_FF)]

    n_chunks = D_FF // FF_CHUNK
    f_acc = jnp.zeros((tm, D_MODEL), F32)
    pending = [up(ci) for ci in range(FF_AHEAD)]
    for ci in range(n_chunks):
        u_pair = pending.pop(0)
        if ci + FF_AHEAD < n_chunks:
            pending.append(up(ci + FF_AHEAD))
        halves = []
        for u, off in zip(u_pair, (0, D_FF)):
            cs = slice(off + ci * FF_CHUNK, off + (ci + 1) * FF_CHUNK)
            tail8 = tail_ref[:, cs]
            cw = cw_ref[:, cs]
            conv = u * cw[FFN_CONV - 1:FFN_CONV, :] + cb_ref[:, cs]
            for kk in range(FFN_CONV - 1):
                conv = conv + _shift_rows(u, tail8, FFN_CONV - 1 - kk) * cw[kk:kk + 1, :]
            tail_ref[:, cs] = u[tm - SUBLANES:tm, :]
            halves.append(conv)
        f = (_silu(halves[0]) * halves[1]).astype(BF16)
        f_acc = f_acc + jnp.dot(f, down_ref[ci * FF_CHUNK:(ci + 1) * FF_CHUNK, :], preferred_element_type=F32)
    o_ref[...] = x1 + _rms(f_acc, g_post_ref[...])


def _out_ffn(x2d, od, ol, os_, wo_d, wo_l, wo_s, g_mix, g_pre, up, cw, cb, down, g_post, tiles_per_seq):
    T, D = x2d.shape
    tm = TM_FFN
    tok = lambda w: pl.BlockSpec((tm, w), lambda t: (t, 0))
    consts = (wo_d, wo_l, wo_s, g_mix, g_pre, up, cw, cb, down, g_post)
    return pl.pallas_call(
        functools.partial(_out_ffn_kernel, tiles_per_seq=tiles_per_seq),
        grid=(T // tm,),
        in_specs=[tok(D), tok(DIFF_W), tok(DIL_W), tok(SSD_INNER)]
                 + [pl.BlockSpec(a.shape, lambda t: (0, 0), pipeline_mode=pl.Buffered(1)) for a in consts],
        out_specs=tok(D),
        out_shape=jax.ShapeDtypeStruct((T, D), F32),
        scratch_shapes=[pltpu.VMEM((SUBLANES, 2 * D_FF), F32)],
        compiler_params=_cparams(("arbitrary",)),
        name="out_ffn",
    )(x2d, od, ol, os_, *consts)


def _rope_tables(S, half):
    inv_freq = jnp.exp(-math.log(ROPE_THETA) * jnp.arange(half, dtype=F32) / half)
    ang = jnp.arange(S, dtype=F32)[:, None] * inv_freq[None, :]
    cos, sin = jnp.cos(ang), jnp.sin(ang)
    zero = jnp.zeros_like(sin)
    reps = LANES // (2 * half)
    cos_t = jnp.tile(jnp.concatenate([cos, cos], axis=1), (1, reps))
    sin_up = jnp.tile(jnp.concatenate([-sin, zero], axis=1), (1, reps))
    sin_dn = jnp.tile(jnp.concatenate([zero, sin], axis=1), (1, reps))
    return cos_t, sin_up, sin_dn


def _rope_tables_t(S, half):
    inv_freq = jnp.exp(-math.log(ROPE_THETA) * jnp.arange(half, dtype=F32) / half)
    ang = jnp.arange(S, dtype=F32)[:, None] * inv_freq[None, :]
    return jnp.cos(ang).T, jnp.sin(ang).T


def kernel(x, pre_mix_norm, w_in, diff_lambda, diff_head_norm, ssd_conv_w, ssd_conv_b, ssd_dt_bias, ssd_A_log,
           ssd_D, ssd_norm, w_out, post_mix_norm, pre_ffn_norm, ffn_up, ffn_conv_w, ffn_conv_b, ffn_down,
           post_ffn_norm):
    B, S, D = x.shape
    depth = w_in.shape[0]
    assert D == D_MODEL and S % DIL_CH == 0 and S % TM_IN == 0 and S % BQ == 0 and S % TM_FFN == 0
    tables = _rope_tables_t(S, DIFF_QK // 2) + _rope_tables(S, DIFF_QK // 2) + _rope_tables(S, DIL_DH // 2)
    tri = (lax.broadcasted_iota(jnp.int32, (SSD_CHUNK, SSD_CHUNK), 1)
           <= lax.broadcasted_iota(jnp.int32, (SSD_CHUNK, SSD_CHUNK), 0)).astype(BF16)
    offs = [0]
    for sz in IN_SIZES:
        offs.append(offs[-1] + sz)

    def pad_lanes(v):
        return jnp.pad(v, ((0, 0), (0, LANES - v.shape[1])))

    for layer in range(depth):
        lambda_init = 0.8 - 0.6 * math.exp(-0.3 * layer)
        wl = w_in[layer].astype(BF16)
        w_a = jnp.concatenate([wl[:, offs[1]:offs[2]], wl[:, offs[3]:offs[5]]], axis=1)
        w_b = jnp.pad(wl[:, offs[5]:], ((0, 0), (0, LANES - SSD_HEADS)))
        ws = [wl[:, offs[0]:offs[1]].T, w_a, w_b, wl[:, offs[2]:offs[3]].T]
        dqt, dk, dvt, lq, lk, lv, z, xbc, dt = _in_proj(x, pre_mix_norm[layer][None, :], tables, ws)

        o_diff = _diff_attn(dqt, dk, dvt, diff_lambda[layer],
                            jnp.broadcast_to(diff_head_norm[layer][:, None], (DIFF_V, LANES)), lambda_init)
        o_dil = _dil_attn(lq, lk, lv)
        o_ssd = _ssd(z, xbc, dt, ssd_conv_w[layer], ssd_conv_b[layer][None, :],
                     pad_lanes(ssd_dt_bias[layer][None, :]), pad_lanes(ssd_A_log[layer][None, :]),
                     jnp.repeat(ssd_D[layer], SSD_P)[None, :], ssd_norm[layer][None, :], tri)

        wo = w_out[layer].astype(BF16)
        x2d = _out_ffn(
            x.reshape(B * S, D), o_diff.reshape(B * S, DIFF_W), o_dil.reshape(B * S, DIL_W),
            o_ssd.reshape(B * S, SSD_INNER),
            wo[0:DIFF_W], wo[DIFF_W:DIFF_W + DIL_W], wo[DIFF_W + DIL_W:],
            post_mix_norm[layer][None, :], pre_ffn_norm[layer][None, :],
            ffn_up[layer].astype(BF16), ffn_conv_w[layer], ffn_conv_b[layer][None, :],
            ffn_down[layer].astype(BF16), post_ffn_norm[layer][None, :], S // TM_FFN)
        x = x2d.reshape(B, S, D)
    return x
```

```python
import functools
import math

import jax
import jax.numpy as jnp
from jax import lax
from jax.experimental import pallas as pl
from jax.experimental.pallas import tpu as pltpu

F32 = jnp.float32
BF16 = jnp.bfloat16

LANES = 128
SUBLANES = 8
VMEM_LIMIT = 56 * 1024 * 1024

D_MODEL = 1024
DIFF_HEADS = 4
DIFF_QK = 32
DIFF_V = 64
DIFF_W = DIFF_HEADS * DIFF_V
DIL_HEADS = 6
DIL_DH = 64
DIL_W = DIL_HEADS * DIL_DH
DIL_PATTERNS = ((128, 1), (512, 4), (2048, 16))
DIL_BLOCK = 128
SSD_HEADS = 6
SSD_P = 64
SSD_GROUPS = 2
SSD_N = 128
SSD_CONV = 4
SSD_CHUNK = 128
SSD_INNER = SSD_HEADS * SSD_P
SSD_XBC = SSD_INNER + 2 * SSD_GROUPS * SSD_N
D_FF = 2816
FFN_CONV = 3
ROPE_THETA = 10000.0
EPS = 1e-6
LOG2E = 1.4426950408889634

IN_SIZES = (DIFF_W, DIFF_W, DIFF_W, DIL_W, DIL_W, DIL_W, SSD_INNER, SSD_XBC, SSD_HEADS)

TM_IN = 512
BQ = 256
BK = 2 * BQ
DIL_CH = 2048
SSD_LC = 512
TM_FFN = 512
FF_CHUNK = 256
FF_AHEAD = 2


def _cparams(sem):
    return pltpu.CompilerParams(dimension_semantics=sem, vmem_limit_bytes=VMEM_LIMIT)


def _const_spec(shape):
    nd = len(shape)
    return pl.BlockSpec(shape, lambda *_: (0,) * nd)


def _silu(x):
    return x * (1.0 / (1.0 + jnp.exp(-x)))


def _split3(x):
    hi = x.astype(BF16)
    r = x - hi.astype(F32)
    mid = r.astype(BF16)
    lo = (r - mid.astype(F32)).astype(BF16)
    return hi, mid, lo


def _rope_lanes(y, cos, sin_up, sin_dn, half):
    outs = []
    for j in range(y.shape[1] // LANES):
        yc = y[:, j * LANES:(j + 1) * LANES]
        up = pltpu.roll(yc, LANES - half, 1)
        dn = pltpu.roll(yc, half, 1)
        outs.append(yc * cos + up * sin_up + dn * sin_dn)
    return jnp.concatenate(outs, axis=1)


def _rope_rows(yt, cos_t, sin_t, half):
    outs = []
    for g in range(yt.shape[0] // (2 * half)):
        x1 = yt[2 * half * g:2 * half * g + half, :]
        x2 = yt[2 * half * g + half:2 * half * (g + 1), :]
        outs += [x1 * cos_t - x2 * sin_t, x2 * cos_t + x1 * sin_t]
    return jnp.concatenate(outs, axis=0)


def _in_proj_kernel(x_ref, g_ref, cdt_ref, sdt_ref, cd_ref, sud_ref, sdd_ref, cl_ref, sul_ref, sdl_ref,
                    wqt_ref, wa_ref, wb_ref, wvt_ref,
                    oqt_ref, ok_ref, ovt_ref, olq_ref, olk_ref, olv_ref, oz_ref, oxbc_ref, odt_ref):
    x = x_ref[...]
    ms = jnp.mean(x * x, axis=-1, keepdims=True)
    h = ((x * lax.rsqrt(ms + EPS)) * g_ref[...]).astype(BF16)

    def proj(w_ref):
        return jnp.dot(h, w_ref[...], preferred_element_type=F32)

    def proj_t(wt_ref):
        return lax.dot_general(wt_ref[...], h, (((1,), (1,)), ((), ())), preferred_element_type=F32)

    r_qt = proj_t(wqt_ref)
    r_a = proj(wa_ref)
    r_b = proj(wb_ref)
    r_vt = proj_t(wvt_ref)

    qt = _rope_rows(r_qt, cdt_ref[...], sdt_ref[...], DIFF_QK // 2)
    oqt_ref[...] = (qt * (DIFF_QK ** -0.5 * LOG2E)).astype(BF16)
    cd, sud, sdd = cd_ref[...], sud_ref[...], sdd_ref[...]
    a0, a1, a2 = DIFF_W, DIFF_W + DIL_W, DIFF_W + 2 * DIL_W
    ok_ref[...] = _rope_lanes(r_a[:, 0:a0], cd, sud, sdd, DIFF_QK // 2).astype(BF16)
    cl, sul, sdl = cl_ref[...], sul_ref[...], sdl_ref[...]
    lq = _rope_lanes(r_a[:, a0:a1], cl, sul, sdl, DIL_DH // 2)
    olq_ref[...] = (lq * (DIL_DH ** -0.5)).astype(BF16)
    olk_ref[...] = _rope_lanes(r_a[:, a1:a2], cl, sul, sdl, DIL_DH // 2).astype(BF16)
    b0, b1, b2 = DIL_W, DIL_W + SSD_INNER, DIL_W + SSD_INNER + SSD_XBC
    olv_ref[...] = r_b[:, 0:b0].astype(BF16)
    oz_ref[...] = r_b[:, b0:b1].astype(BF16)
    oxbc_ref[...] = r_b[:, b1:b2]
    odt_ref[...] = r_b[:, b2:b2 + LANES]
    ovt_ref[...] = r_vt.astype(BF16)


def _in_proj(x, gain, tables, ws):
    B, S, D = x.shape
    tm = TM_IN
    grid = (S // tm, B)
    tok = lambda w: pl.BlockSpec((None, tm, w), lambda s, b: (b, s, 0))
    tok_t = lambda w: pl.BlockSpec((None, w, tm), lambda s, b: (b, 0, s))
    tab = pl.BlockSpec((tm, LANES), lambda s, b: (s, 0))
    tab_t = pl.BlockSpec((DIFF_QK // 2, tm), lambda s, b: (0, s))
    out_widths = (DIFF_W, DIFF_W, DIFF_W, DIL_W, DIL_W, DIL_W, SSD_INNER, SSD_XBC, LANES)
    out_dtypes = (BF16, BF16, BF16, BF16, BF16, BF16, BF16, F32, F32)
    transposed = (True, False, True) + (False,) * 6
    return pl.pallas_call(
        _in_proj_kernel,
        grid=grid,
        in_specs=[tok(D), _const_spec((1, D))] + [tab_t] * 2 + [tab] * 6 + [_const_spec(w.shape) for w in ws],
        out_specs=[tok_t(w) if t else tok(w) for w, t in zip(out_widths, transposed)],
        out_shape=[jax.ShapeDtypeStruct((B, w, S) if t else (B, S, w), dt)
                   for w, dt, t in zip(out_widths, out_dtypes, transposed)],
        compiler_params=_cparams(("arbitrary", "arbitrary")),
        name="in_proj",
    )(x, gain, *tables, *ws)


N_SM = 2 * DIFF_HEADS
ACC_ROWS = DIFF_V + 16
AHEAD = 0


def _diff_attn_kernel(lam_ref, gain_ref, qt_ref, k_ref, vt_ref, o_ref, m_ref, acc_ref, sa_ref, sb_ref, *,
                      lambda_init):
    i = pl.program_id(1)
    bq = qt_ref.shape[1]
    m_ref[...] = jnp.full(m_ref.shape, -jnp.inf, F32)
    acc_ref[...] = jnp.zeros(acc_ref.shape, F32)
    bk = sa_ref.shape[0]
    ones = jnp.ones((ACC_ROWS - DIFF_V, bk), BF16)

    def key_tile(j):
        return k_ref[pl.ds(pl.multiple_of(j * bk, bk), bk), :]

    def scores_into(dst_ref, kj, g):
        per_col = LANES // DIFF_QK
        col, r = g // per_col, g % per_col
        parts = [qt_ref[g * DIFF_QK:(g + 1) * DIFF_QK, :]]
        if r > 0:
            parts.insert(0, jnp.zeros((r * DIFF_QK, bq), BF16))
        if r < per_col - 1:
            parts.append(jnp.zeros(((per_col - 1 - r) * DIFF_QK, bq), BF16))
        rhs = jnp.concatenate(parts, axis=0)
        dst_ref[:, g * bq:(g + 1) * bq] = jnp.dot(kj[:, col * LANES:(col + 1) * LANES], rhs,
                                                  preferred_element_type=F32)

    def step(src_ref, j, diag, dst_ref):
        vtj = vt_ref[:, pl.ds(pl.multiple_of(j * bk, bk), bk)]
        masked = diag is not None
        if masked:
            key = lax.broadcasted_iota(jnp.int32, (bk, bq), 0)
            qry = lax.broadcasted_iota(jnp.int32, (bk, bq), 1) + diag
            visible = key <= qry
        if dst_ref is not None:
            kn = key_tile(j + 1)
            for g in range(AHEAD):
                scores_into(dst_ref, kn, g)
        m_all = m_ref[...]
        new_m, new_acc = [], []
        for g in range(N_SM):
            s = src_ref[:, g * bq:(g + 1) * bq]
            if masked:
                s = jnp.where(visible, s, -jnp.inf)
            m_prev = m_all[:, g * bq:(g + 1) * bq]
            m_new = jnp.maximum(m_prev, jnp.max(s, axis=0, keepdims=True))
            alpha = jnp.exp2(m_prev - m_new)
            p = jnp.exp2(s - m_new).astype(BF16)
            if dst_ref is not None and g + AHEAD < N_SM:
                scores_into(dst_ref, kn, g + AHEAD)
            hd = g // 2
            lhs = jnp.concatenate([vtj[hd * DIFF_V:(hd + 1) * DIFF_V, :], ones], axis=0)
            pv = jnp.dot(lhs, p, preferred_element_type=F32)
            new_m.append(m_new)
            new_acc.append(acc_ref[g] * alpha + pv)
        m_ref[...] = jnp.concatenate(new_m, axis=1)
        for g in range(N_SM):
            acc_ref[g] = new_acc[g]

    k0 = key_tile(0)
    for g in range(N_SM):
        scores_into(sa_ref, k0, g)

    n = i // 2
    q_off = (i % 2) * bq

    def body(jj, carry):
        step(sa_ref, 2 * jj, None, sb_ref)
        step(sb_ref, 2 * jj + 1, None, sa_ref)
        return carry

    lax.fori_loop(0, n // 2, body, 0)

    @pl.when(n % 2 == 0)
    def _():
        step(sa_ref, n, q_off, None)

    @pl.when(n % 2 == 1)
    def _():
        step(sa_ref, n - 1, None, sb_ref)
        step(sb_ref, n, q_off, None)

    lam_p = lam_ref[...]
    s1 = jnp.sum(lam_p[0:1, :] * lam_p[1:2, :], axis=1, keepdims=True)
    s2 = jnp.sum(lam_p[2:3, :] * lam_p[3:4, :], axis=1, keepdims=True)
    lam = jnp.exp(s1) - jnp.exp(s2) + lambda_init
    gain = jnp.tile(gain_ref[...], (1, bq // LANES))
    outs = []
    for hd in range(DIFF_HEADS):
        a1 = acc_ref[2 * hd]
        a2 = acc_ref[2 * hd + 1]
        o1 = a1[0:DIFF_V, :] / a1[DIFF_V:DIFF_V + 1, :]
        o2 = a2[0:DIFF_V, :] / a2[DIFF_V:DIFF_V + 1, :]
        d = o1 - lam * o2
        ms = jnp.mean(d * d, axis=0, keepdims=True)
        outs.append(((d * lax.rsqrt(ms + EPS)) * gain) * (1.0 - lambda_init))
    o_ref[...] = jnp.concatenate(outs, axis=0).T.astype(o_ref.dtype)


def _diff_attn(qt, k, vt, lam_p, gain_b, lambda_init):
    B, S, W = k.shape
    grid = (B, S // BQ)
    return pl.pallas_call(
        functools.partial(_diff_attn_kernel, lambda_init=lambda_init),
        grid=grid,
        in_specs=[_const_spec(lam_p.shape), _const_spec(gain_b.shape),
                  pl.BlockSpec((None, W, BQ), lambda b, i: (b, 0, i)),
                  pl.BlockSpec((None, S, W), lambda b, i: (b, 0, 0)),
                  pl.BlockSpec((None, W, S), lambda b, i: (b, 0, 0))],
        out_specs=pl.BlockSpec((None, BQ, W), lambda b, i: (b, i, 0)),
        out_shape=jax.ShapeDtypeStruct((B, S, W), BF16),
        scratch_shapes=[pltpu.VMEM((1, N_SM * BQ), F32),
                        pltpu.VMEM((N_SM, ACC_ROWS, BQ), F32),
                        pltpu.VMEM((BK, N_SM * BQ), F32),
                        pltpu.VMEM((BK, N_SM * BQ), F32)],
        compiler_params=_cparams(("arbitrary", "arbitrary")),
        name="diff_attn",
    )(lam_p, gain_b, qt, k, vt)


N_PAIRS = DIL_W // LANES
DIL_UNROLL = 2


def _dil_block(qs, ks, vs, biases):
    nq = DIL_BLOCK
    low = lax.broadcasted_iota(jnp.int32, (nq, LANES), 1) < DIL_DH
    scores = []
    for qp, kp, bias in zip(qs, ks, biases):
        q2 = jnp.concatenate([jnp.where(low, qp, 0.0), jnp.where(low, 0.0, qp)], axis=0).astype(BF16)
        scores.append(lax.dot_general(q2, kp.astype(BF16), (((1,), (1,)), ((), ())),
                                      preferred_element_type=F32) + bias)
    out = []
    for s, vp in zip(scores, vs):
        m = jnp.max(s, axis=1, keepdims=True)
        p = jnp.exp(s - m)
        den = jnp.sum(p, axis=1, keepdims=True)
        num = jnp.dot(p.astype(BF16), vp.astype(BF16), preferred_element_type=F32)
        out.append(tuple(jnp.where(low, t[0:nq], t[nq:2 * nq]) for t in (num, den, m)))
    return out


def _dil_merge(a, b):
    (n1, d1, m1), (n2, d2, m2) = a, b
    m = jnp.maximum(m1, m2)
    w1 = jnp.exp(m1 - m)
    w2 = jnp.exp(m2 - m)
    return n1 * w1 + n2 * w2, d1 * w1 + d2 * w2, m


def _dil_attn_kernel(q_ref, k_ref, v_ref, o_ref, qf_ref, kk_ref, vv_ref, num_ref, den_ref, max_ref, bias_ref):
    c = pl.program_id(1)
    ch = q_ref.shape[0]

    iq = lax.broadcasted_iota(jnp.int32, (2 * DIL_BLOCK, 2 * DIL_BLOCK), 0) % DIL_BLOCK
    jk = lax.broadcasted_iota(jnp.int32, (2 * DIL_BLOCK, 2 * DIL_BLOCK), 1)
    band = (jk >= iq) & (jk <= iq + DIL_BLOCK)
    bias_ref[0] = jnp.where(band, 0.0, -jnp.inf)
    bias_ref[1] = jnp.where(band & (jk >= DIL_BLOCK), 0.0, -jnp.inf)

    @pl.when(c == 0)
    def _():
        kk_ref[:, 0:ch, :] = jnp.zeros((N_PAIRS, ch, LANES), F32)
        vv_ref[:, 0:ch, :] = jnp.zeros((N_PAIRS, ch, LANES), F32)

    for pr in range(N_PAIRS):
        sl = slice(pr * LANES, (pr + 1) * LANES)
        qf_ref[pr] = q_ref[:, sl].astype(F32)
        kk_ref[pr, ch:2 * ch, :] = k_ref[:, sl].astype(F32)
        vv_ref[pr, ch:2 * ch, :] = v_ref[:, sl].astype(F32)
    first_chunk = c == 0

    def blocks(specs):
        qs, ks, vs, biases = [], [], [], []
        for rows, keys, first_block in specs:
            bias = bias_ref[first_block.astype(jnp.int32)]
            for pr in range(N_PAIRS):
                qs.append(qf_ref[pr, rows, :])
                ks.append(kk_ref[pr, keys, :])
                vs.append(vv_ref[pr, keys, :])
                biases.append(bias)
        res = _dil_block(qs, ks, vs, biases)
        return [res[t * N_PAIRS:(t + 1) * N_PAIRS] for t in range(len(specs))]

    def merged(pr, rows, new):
        return _dil_merge((num_ref[pr, rows, :], den_ref[pr, rows, :], max_ref[pr, rows, :]), new)

    def store(pr, rows, ndm):
        num_ref[pr, rows, :] = ndm[0]
        den_ref[pr, rows, :] = ndm[1]
        max_ref[pr, rows, :] = ndm[2]

    d16 = DIL_PATTERNS[2][1]

    def body16(rr, carry):
        specs = [(pl.ds(DIL_UNROLL * rr + u, DIL_BLOCK, stride=d16),
                  pl.ds(DIL_UNROLL * rr + u, 2 * DIL_BLOCK, stride=d16), first_chunk) for u in range(DIL_UNROLL)]
        for (rows, _, _), res in zip(specs, blocks(specs)):
            for pr, new in enumerate(res):
                store(pr, rows, new)
        return carry

    lax.fori_loop(0, d16 // DIL_UNROLL, body16, 0)

    d4 = DIL_PATTERNS[1][1]
    blocks4 = ch // (d4 * DIL_BLOCK)

    def body4(tt, carry):
        nb = (DIL_UNROLL * tt) // d4
        first_block = jnp.logical_and(first_chunk, nb == 0)
        specs = []
        for u in range(DIL_UNROLL):
            q0 = (DIL_UNROLL * tt) % d4 + u + nb * (d4 * DIL_BLOCK)
            specs.append((pl.ds(q0, DIL_BLOCK, stride=d4),
                          pl.ds(ch + q0 - d4 * DIL_BLOCK, 2 * DIL_BLOCK, stride=d4), first_block))
        for (rows, _, _), res in zip(specs, blocks(specs)):
            for pr, new in enumerate(res):
                store(pr, rows, merged(pr, rows, new))
        return carry

    lax.fori_loop(0, d4 * blocks4 // DIL_UNROLL, body4, 0)

    def body1(nn, carry):
        specs = []
        for u in range(DIL_UNROLL):
            nb = DIL_UNROLL * nn + u
            q0 = pl.multiple_of(nb * DIL_BLOCK, DIL_BLOCK)
            specs.append((pl.ds(q0, DIL_BLOCK),
                          pl.ds(pl.multiple_of(ch + q0 - DIL_BLOCK, DIL_BLOCK), 2 * DIL_BLOCK),
                          jnp.logical_and(first_chunk, nb == 0)))
        for (rows, _, _), res in zip(specs, blocks(specs)):
            for pr, new in enumerate(res):
                n, d, _ = merged(pr, rows, new)
                o_ref[rows, pr * LANES:(pr + 1) * LANES] = (n / d).astype(o_ref.dtype)
        return carry

    lax.fori_loop(0, ch // DIL_BLOCK // DIL_UNROLL, body1, 0)

    kk_ref[:, 0:ch, :] = kk_ref[:, ch:2 * ch, :]
    vv_ref[:, 0:ch, :] = vv_ref[:, ch:2 * ch, :]


def _dil_attn(q, k, v):
    B, S, W = q.shape
    ch = DIL_CH
    blk = pl.BlockSpec((None, ch, W), lambda b, c: (b, c, 0))
    return pl.pallas_call(
        _dil_attn_kernel,
        grid=(B, S // ch),
        in_specs=[blk, blk, blk],
        out_specs=blk,
        out_shape=jax.ShapeDtypeStruct((B, S, W), BF16),
        scratch_shapes=[pltpu.VMEM((N_PAIRS, ch, LANES), F32), pltpu.VMEM((N_PAIRS, 2 * ch, LANES), F32),
                        pltpu.VMEM((N_PAIRS, 2 * ch, LANES), F32), pltpu.VMEM((N_PAIRS, ch, LANES), F32),
                        pltpu.VMEM((N_PAIRS, ch, LANES), F32), pltpu.VMEM((N_PAIRS, ch, LANES), F32),
                        pltpu.VMEM((2, 2 * DIL_BLOCK, 2 * DIL_BLOCK), F32)],
        compiler_params=_cparams(("arbitrary", "arbitrary")),
        name="dil_attn",
    )(q, k, v)


def _shift_rows(x, tail8, s):
    n = x.shape[0]
    row = lax.broadcasted_iota(jnp.int32, x.shape, 0)
    head = jnp.tile(pltpu.roll(tail8, s, 0), (n // SUBLANES, 1))
    return jnp.where(row < s, head, pltpu.roll(x, s, 0))


def _expand_heads(v, n_heads, width):
    rows = v.shape[0]
    lane = lax.broadcasted_iota(jnp.int32, (rows, n_heads * width), 1)
    out = jnp.zeros((rows, n_heads * width), F32)
    for h in range(n_heads):
        out = jnp.where(lane // width == h, v[:, h:h + 1], out)
    return out


def _ssd_kernel(z_ref, xbc_ref, dt_ref, cw_ref, cb_ref, dtb_ref, alog_ref, dvec_ref, gain_ref, tri_ref,
                o_ref, state_ref, tail_ref):
    c = pl.program_id(1)
    lc = xbc_ref.shape[0]
    L = SSD_CHUNK

    @pl.when(c == 0)
    def _():
        state_ref[...] = jnp.zeros(state_ref.shape, F32)
        tail_ref[...] = jnp.zeros(tail_ref.shape, F32)

    xbc = xbc_ref[...]
    tail8 = tail_ref[...]
    cw = cw_ref[...]
    acc = xbc * cw[SSD_CONV - 1:SSD_CONV, :] + cb_ref[...]
    for kk in range(SSD_CONV - 1):
        acc = acc + _shift_rows(xbc, tail8, SSD_CONV - 1 - kk) * cw[kk:kk + 1, :]
    tail_ref[...] = xbc[lc - SUBLANES:lc, :]
    xbc_c = _silu(acc)

    xs_all = xbc_c[:, 0:SSD_INNER]
    dt_raw = dt_ref[...] + dtb_ref[...]
    dt_all = jnp.maximum(dt_raw, 0.0) + jnp.log(1.0 + jnp.exp(-jnp.abs(dt_raw)))
    a_all = dt_all * (-jnp.exp(alog_ref[...]))
    tri = tri_ref[...]
    lane_in = lax.broadcasted_iota(jnp.int32, (L, SSD_INNER), 1)
    grp_w = SSD_INNER // SSD_GROUPS
    ii = lax.broadcasted_iota(jnp.int32, (L, L), 0)
    jj = lax.broadcasted_iota(jnp.int32, (L, L), 1)
    causal = jj <= ii

    for ck in range(lc // L):
        rs = slice(ck * L, (ck + 1) * L)
        xs = xs_all[rs, :]
        a = a_all[rs, :]
        hi, mid, lo = _split3(a)
        acs = (jnp.dot(tri, hi, preferred_element_type=F32) + jnp.dot(tri, mid, preferred_element_type=F32)
               + jnp.dot(tri, lo, preferred_element_type=F32))
        acs_t = acs.T
        dt_e = _expand_heads(dt_all[rs, :], SSD_HEADS, SSD_P)
        acs_e = _expand_heads(acs, SSD_HEADS, SSD_P)
        last_e = acs_e[L - 1:L, :]
        xdt = xs * dt_e
        xdt_b = xdt.astype(BF16)
        xdtd_b = (xdt * jnp.exp(last_e - acs_e)).astype(BF16)
        prev = state_ref[...]
        prev_b = prev.astype(BF16)
        y = xs * dvec_ref[...]
        new_state = prev * jnp.exp(last_e)
        y_off = jnp.zeros((L, SSD_INNER), F32)
        for g in range(SSD_GROUPS):
            bg = xbc_c[rs, SSD_INNER + g * SSD_N:SSD_INNER + (g + 1) * SSD_N]
            cg = xbc_c[rs, SSD_INNER + (SSD_GROUPS + g) * SSD_N:SSD_INNER + (SSD_GROUPS + g + 1) * SSD_N]
            bg_b = bg.astype(BF16)
            cg_b = cg.astype(BF16)
            in_grp = lane_in // grp_w == g
            cb = lax.dot_general(cg_b, bg_b, (((1,), (1,)), ((), ())), preferred_element_type=F32)
            for hh in range(SSD_HEADS // SSD_GROUPS):
                hd = g * (SSD_HEADS // SSD_GROUPS) + hh
                seg = acs[:, hd:hd + 1] - acs_t[hd:hd + 1, :]
                w = jnp.where(causal, cb * jnp.exp(seg), 0.0).astype(BF16)
                y = y + jnp.dot(w, jnp.where(lane_in // SSD_P == hd, xdt_b, jnp.zeros_like(xdt_b)),
                                preferred_element_type=F32)
            zero_b = jnp.zeros_like(xdtd_b)
            new_state = new_state + jnp.dot(bg.T.astype(BF16), jnp.where(in_grp, xdtd_b, zero_b),
                                            preferred_element_type=F32)
            y_off = y_off + jnp.dot(cg_b, jnp.where(in_grp, prev_b, jnp.zeros_like(prev_b)),
                                    preferred_element_type=F32)
        state_ref[...] = new_state
        y = y + y_off * jnp.exp(acs_e)
        y = y * _silu(z_ref[rs, :].astype(F32))
        ysq = y * y
        in0 = lane_in < grp_w
        ms0 = jnp.sum(jnp.where(in0, ysq, 0.0), axis=1, keepdims=True) * (1.0 / grp_w)
        ms1 = jnp.sum(jnp.where(in0, 0.0, ysq), axis=1, keepdims=True) * (1.0 / grp_w)
        r = jnp.where(in0, lax.rsqrt(ms0 + EPS), lax.rsqrt(ms1 + EPS))
        o_ref[rs, :] = ((y * r) * gain_ref[...]).astype(o_ref.dtype)


def _ssd(z, xbc, dt, cw, cb, dtb, alog, dvec, gain, tri):
    B, S, _ = z.shape
    lc = SSD_LC
    tok = lambda w: pl.BlockSpec((None, lc, w), lambda b, c: (b, c, 0))
    consts = (cw, cb, dtb, alog, dvec, gain, tri)
    return pl.pallas_call(
        _ssd_kernel,
        grid=(B, S // lc),
        in_specs=[tok(SSD_INNER), tok(SSD_XBC), tok(LANES)] + [_const_spec(a.shape) for a in consts],
        out_specs=tok(SSD_INNER),
        out_shape=jax.ShapeDtypeStruct((B, S, SSD_INNER), BF16),
        scratch_shapes=[pltpu.VMEM((SSD_N, SSD_INNER), F32), pltpu.VMEM((SUBLANES, SSD_XBC), F32)],
        compiler_params=_cparams(("arbitrary", "arbitrary")),
        name="ssd",
    )(z, xbc, dt, *consts)


def _rms(x, gain):
    ms = jnp.mean(x * x, axis=-1, keepdims=True)
    return (x * lax.rsqrt(ms + EPS)) * gain


def _out_ffn_kernel(x_ref, od_ref, ol_ref, os_ref, wo_d_ref, wo_l_ref, wo_s_ref, g_mix_ref, g_pre_ref,
                    up_ref, cw_ref, cb_ref, down_ref, g_post_ref, o_ref, tail_ref, *, tiles_per_seq):
    t = pl.program_id(0)
    tm = x_ref.shape[0]

    @pl.when(t % tiles_per_seq == 0)
    def _():
        tail_ref[...] = jnp.zeros(tail_ref.shape, F32)

    mix = (jnp.dot(od_ref[...], wo_d_ref[...], preferred_element_type=F32)
           + jnp.dot(ol_ref[...], wo_l_ref[...], preferred_element_type=F32)
           + jnp.dot(os_ref[...], wo_s_ref[...], preferred_element_type=F32))
    x1 = x_ref[...] + _rms(mix, g_mix_ref[...])
    h = _rms(x1, g_pre_ref[...]).astype(BF16)

    def up(ci):
        return [jnp.dot(h, up_ref[:, off + ci * FF_CHUNK:off + (ci + 1) * FF_CHUNK], preferred_element_type=F32)
                for off in (0, D_FF)]

    n_chunks = D_FF // FF_CHUNK
    f_acc = jnp.zeros((tm, D_MODEL), F32)
    pending = [up(ci) for ci in range(FF_AHEAD)]
    for ci in range(n_chunks):
        u_pair = pending.pop(0)
        if ci + FF_AHEAD < n_chunks:
            pending.append(up(ci + FF_AHEAD))
        halves = []
        for u, off in zip(u_pair, (0, D_FF)):
            cs = slice(off + ci * FF_CHUNK, off + (ci + 1) * FF_CHUNK)
            tail8 = tail_ref[:, cs]
            cw = cw_ref[:, cs]
            conv = u * cw[FFN_CONV - 1:FFN_CONV, :] + cb_ref[:, cs]
            for kk in range(FFN_CONV - 1):
                conv = conv + _shift_rows(u, tail8, FFN_CONV - 1 - kk) * cw[kk:kk + 1, :]
            tail_ref[:, cs] = u[tm - SUBLANES:tm, :]
            halves.append(conv)
        f = (_silu(halves[0]) * halves[1]).astype(BF16)
        f_acc = f_acc + jnp.dot(f, down_ref[ci * FF_CHUNK:(ci + 1) * FF_CHUNK, :], preferred_element_type=F32)
    o_ref[...] = x1 + _rms(f_acc, g_post_ref[...])


def _out_ffn(x2d, od, ol, os_, wo_d, wo_l, wo_s, g_mix, g_pre, up, cw, cb, down, g_post, tiles_per_seq):
    T, D = x2d.shape
    tm = TM_FFN
    tok = lambda w: pl.BlockSpec((tm, w), lambda t: (t, 0))
    consts = (wo_d, wo_l, wo_s, g_mix, g_pre, up, cw, cb, down, g_post)
    return pl.pallas_call(
        functools.partial(_out_ffn_kernel, tiles_per_seq=tiles_per_seq),
        grid=(T // tm,),
        in_specs=[tok(D), tok(DIFF_W), tok(DIL_W), tok(SSD_INNER)]
                 + [pl.BlockSpec(a.shape, lambda t: (0, 0), pipeline_mode=pl.Buffered(1)) for a in consts],
        out_specs=tok(D),
        out_shape=jax.ShapeDtypeStruct((T, D), F32),
        scratch_shapes=[pltpu.VMEM((SUBLANES, 2 * D_FF), F32)],
        compiler_params=_cparams(("arbitrary",)),
        name="out_ffn",
    )(x2d, od, ol, os_, *consts)


def _rope_tables(S, half):
    inv_freq = jnp.exp(-math.log(ROPE_THETA) * jnp.arange(half, dtype=F32) / half)
    ang = jnp.arange(S, dtype=F32)[:, None] * inv_freq[None, :]
    cos, sin = jnp.cos(ang), jnp.sin(ang)
    zero = jnp.zeros_like(sin)
    reps = LANES // (2 * half)
    cos_t = jnp.tile(jnp.concatenate([cos, cos], axis=1), (1, reps))
    sin_up = jnp.tile(jnp.concatenate([-sin, zero], axis=1), (1, reps))
    sin_dn = jnp.tile(jnp.concatenate([zero, sin], axis=1), (1, reps))
    return cos_t, sin_up, sin_dn


def _rope_tables_t(S, half):
    inv_freq = jnp.exp(-math.log(ROPE_THETA) * jnp.arange(half, dtype=F32) / half)
    ang = jnp.arange(S, dtype=F32)[:, None] * inv_freq[None, :]
    return jnp.cos(ang).T, jnp.sin(ang).T


def kernel(x, pre_mix_norm, w_in, diff_lambda, diff_head_norm, ssd_conv_w, ssd_conv_b, ssd_dt_bias, ssd_A_log,
           ssd_D, ssd_norm, w_out, post_mix_norm, pre_ffn_norm, ffn_up, ffn_conv_w, ffn_conv_b, ffn_down,
           post_ffn_norm):
    B, S, D = x.shape
    depth = w_in.shape[0]
    assert D == D_MODEL and S % DIL_CH == 0 and S % TM_IN == 0 and S % BQ == 0 and S % TM_FFN == 0
    tables = _rope_tables_t(S, DIFF_QK // 2) + _rope_tables(S, DIFF_QK // 2) + _rope_tables(S, DIL_DH // 2)
    tri = (lax.broadcasted_iota(jnp.int32, (SSD_CHUNK, SSD_CHUNK), 1)
           <= lax.broadcasted_iota(jnp.int32, (SSD_CHUNK, SSD_CHUNK), 0)).astype(BF16)
    offs = [0]
    for sz in IN_SIZES:
        offs.append(offs[-1] + sz)

    def pad_lanes(v):
        return jnp.pad(v, ((0, 0), (0, LANES - v.shape[1])))

    for layer in range(depth):
        lambda_init = 0.8 - 0.6 * math.exp(-0.3 * layer)
        wl = w_in[layer].astype(BF16)
        w_a = jnp.concatenate([wl[:, offs[1]:offs[2]], wl[:, offs[3]:offs[5]]], axis=1)
        w_b = jnp.pad(wl[:, offs[5]:], ((0, 0), (0, LANES - SSD_HEADS)))
        ws = [wl[:, offs[0]:offs[1]].T, w_a, w_b, wl[:, offs[2]:offs[3]].T]
        dqt, dk, dvt, lq, lk, lv, z, xbc, dt = _in_proj(x, pre_mix_norm[layer][None, :], tables, ws)

        o_diff = _diff_attn(dqt, dk, dvt, diff_lambda[layer],
                            jnp.broadcast_to(diff_head_norm[layer][:, None], (DIFF_V, LANES)), lambda_init)
        o_dil = _dil_attn(lq, lk, lv)
        o_ssd = _ssd(z, xbc, dt, ssd_conv_w[layer], ssd_conv_b[layer][None, :],
                     pad_lanes(ssd_dt_bias[layer][None, :]), pad_lanes(ssd_A_log[layer][None, :]),
                     jnp.repeat(ssd_D[layer], SSD_P)[None, :], ssd_norm[layer][None, :], tri)

        wo = w_out[layer].astype(BF16)
        x2d = _out_ffn(
            x.reshape(B * S, D), o_diff.reshape(B * S, DIFF_W), o_dil.reshape(B * S, DIL_W),
            o_ssd.reshape(B * S, SSD_INNER),
            wo[0:DIFF_W], wo[DIFF_W:DIFF_W + DIL_W], wo[DIFF_W + DIL_W:],
            post_mix_norm[layer][None, :], pre_ffn_norm[layer][None, :],
            ffn_up[layer].astype(BF16), ffn_conv_w[layer], ffn_conv_b[layer][None, :],
            ffn_down[layer].astype(BF16), post_ffn_norm[layer][None, :], S // TM_FFN)
        x = x2d.reshape(B, S, D)
    return x
```

```python
import functools
import math

import jax
import jax.numpy as jnp
from jax import lax
from jax.experimental import pallas as pl
from jax.experimental.pallas import tpu as pltpu

F32 = jnp.float32
BF16 = jnp.bfloat16

LANES = 128
SUBLANES = 8
VMEM_LIMIT = 56 * 1024 * 1024

D_MODEL = 1024
DIFF_HEADS = 4
DIFF_QK = 32
DIFF_V = 64
DIFF_W = DIFF_HEADS * DIFF_V
DIL_HEADS = 6
DIL_DH = 64
DIL_W = DIL_HEADS * DIL_DH
DIL_PATTERNS = ((128, 1), (512, 4), (2048, 16))
DIL_BLOCK = 128
SSD_HEADS = 6
SSD_P = 64
SSD_GROUPS = 2
SSD_N = 128
SSD_CONV = 4
SSD_CHUNK = 128
SSD_INNER = SSD_HEADS * SSD_P
SSD_XBC = SSD_INNER + 2 * SSD_GROUPS * SSD_N
D_FF = 2816
FFN_CONV = 3
ROPE_THETA = 10000.0
EPS = 1e-6
LOG2E = 1.4426950408889634

IN_SIZES = (DIFF_W, DIFF_W, DIFF_W, DIL_W, DIL_W, DIL_W, SSD_INNER, SSD_XBC, SSD_HEADS)

TM_IN = 512
BQ = 256
BK = 2 * BQ
DIL_CH = 2048
SSD_LC = 512
TM_FFN = 256
FF_CHUNK = 256
FF_AHEAD = 2


def _cparams(sem):
    return pltpu.CompilerParams(dimension_semantics=sem, vmem_limit_bytes=VMEM_LIMIT)


def _const_spec(shape):
    nd = len(shape)
    return pl.BlockSpec(shape, lambda *_: (0,) * nd)


def _silu(x):
    return x * (1.0 / (1.0 + jnp.exp(-x)))


def _split3(x):
    hi = x.astype(BF16)
    r = x - hi.astype(F32)
    mid = r.astype(BF16)
    lo = (r - mid.astype(F32)).astype(BF16)
    return hi, mid, lo


def _rope_lanes(y, cos, sin_up, sin_dn, half):
    outs = []
    for j in range(y.shape[1] // LANES):
        yc = y[:, j * LANES:(j + 1) * LANES]
        up = pltpu.roll(yc, LANES - half, 1)
        dn = pltpu.roll(yc, half, 1)
        outs.append(yc * cos + up * sin_up + dn * sin_dn)
    return jnp.concatenate(outs, axis=1)


def _rope_rows(yt, cos_t, sin_t, half):
    outs = []
    for g in range(yt.shape[0] // (2 * half)):
        x1 = yt[2 * half * g:2 * half * g + half, :]
        x2 = yt[2 * half * g + half:2 * half * (g + 1), :]
        outs += [x1 * cos_t - x2 * sin_t, x2 * cos_t + x1 * sin_t]
    return jnp.concatenate(outs, axis=0)


def _in_proj_kernel(x_ref, g_ref, cdt_ref, sdt_ref, cd_ref, sud_ref, sdd_ref, cl_ref, sul_ref, sdl_ref,
                    wt_ref, wa_ref, wb_ref,
                    oqt_ref, ok_ref, ovt_ref, olq_ref, olk_ref, olv_ref, oz_ref, oxbc_ref, odt_ref):
    x = x_ref[...]
    ms = jnp.mean(x * x, axis=-1, keepdims=True)
    h = ((x * lax.rsqrt(ms + EPS)) * g_ref[...]).astype(BF16)

    def proj(w_ref):
        return jnp.dot(h, w_ref[...], preferred_element_type=F32)

    def proj_t(wt_ref):
        return lax.dot_general(wt_ref[...], h, (((1,), (1,)), ((), ())), preferred_element_type=F32)

    r_t = proj_t(wt_ref)
    r_qt, r_vt = r_t[0:DIFF_W, :], r_t[DIFF_W:2 * DIFF_W, :]
    r_a = proj(wa_ref)
    r_b = proj(wb_ref)

    qt = _rope_rows(r_qt, cdt_ref[...], sdt_ref[...], DIFF_QK // 2)
    oqt_ref[...] = (qt * (DIFF_QK ** -0.5 * LOG2E)).astype(BF16)
    cd, sud, sdd = cd_ref[...], sud_ref[...], sdd_ref[...]
    a0, a1, a2 = DIFF_W, DIFF_W + DIL_W, DIFF_W + 2 * DIL_W
    ok_ref[...] = _rope_lanes(r_a[:, 0:a0], cd, sud, sdd, DIFF_QK // 2).astype(BF16)
    cl, sul, sdl = cl_ref[...], sul_ref[...], sdl_ref[...]
    lq = _rope_lanes(r_a[:, a0:a1], cl, sul, sdl, DIL_DH // 2)
    olq_ref[...] = (lq * (DIL_DH ** -0.5)).astype(BF16)
    olk_ref[...] = _rope_lanes(r_a[:, a1:a2], cl, sul, sdl, DIL_DH // 2).astype(BF16)
    b0, b1, b2 = DIL_W, DIL_W + SSD_INNER, DIL_W + SSD_INNER + SSD_XBC
    olv_ref[...] = r_b[:, 0:b0].astype(BF16)
    oz_ref[...] = r_b[:, b0:b1].astype(BF16)
    oxbc_ref[...] = r_b[:, b1:b2]
    odt_ref[...] = r_b[:, b2:b2 + LANES]
    ovt_ref[...] = r_vt.astype(BF16)


def _in_proj(x, gain, tables, ws):
    B, S, D = x.shape
    tm = TM_IN
    grid = (S // tm, B)
    tok = lambda w: pl.BlockSpec((None, tm, w), lambda s, b: (b, s, 0))
    tok_t = lambda w: pl.BlockSpec((None, w, tm), lambda s, b: (b, 0, s))
    tab = pl.BlockSpec((tm, LANES), lambda s, b: (s, 0))
    tab_t = pl.BlockSpec((DIFF_QK // 2, tm), lambda s, b: (0, s))
    out_widths = (DIFF_W, DIFF_W, DIFF_W, DIL_W, DIL_W, DIL_W, SSD_INNER, SSD_XBC, LANES)
    out_dtypes = (BF16, BF16, BF16, BF16, BF16, BF16, BF16, F32, F32)
    transposed = (True, False, True) + (False,) * 6
    return pl.pallas_call(
        _in_proj_kernel,
        grid=grid,
        in_specs=[tok(D), _const_spec((1, D))] + [tab_t] * 2 + [tab] * 6 + [_const_spec(w.shape) for w in ws],
        out_specs=[tok_t(w) if t else tok(w) for w, t in zip(out_widths, transposed)],
        out_shape=[jax.ShapeDtypeStruct((B, w, S) if t else (B, S, w), dt)
                   for w, dt, t in zip(out_widths, out_dtypes, transposed)],
        compiler_params=_cparams(("arbitrary", "arbitrary")),
        name="in_proj",
    )(x, gain, *tables, *ws)


N_SM = 2 * DIFF_HEADS
ACC_ROWS = DIFF_V + 16
AHEAD = 0


def _diff_attn_kernel(lam_ref, gain_ref, qt_ref, k_ref, vt_ref, o_ref, m_ref, acc_ref, sa_ref, sb_ref, *,
                      lambda_init):
    i = pl.program_id(1)
    bq = qt_ref.shape[1]
    m_ref[...] = jnp.full(m_ref.shape, -jnp.inf, F32)
    acc_ref[...] = jnp.zeros(acc_ref.shape, F32)
    bk = sa_ref.shape[0]
    ones = jnp.ones((ACC_ROWS - DIFF_V, bk), BF16)

    def key_tile(j):
        return k_ref[pl.ds(pl.multiple_of(j * bk, bk), bk), :]

    def scores_into(dst_ref, kj, g):
        per_col = LANES // DIFF_QK
        col, r = g // per_col, g % per_col
        parts = [qt_ref[g * DIFF_QK:(g + 1) * DIFF_QK, :]]
        if r > 0:
            parts.insert(0, jnp.zeros((r * DIFF_QK, bq), BF16))
        if r < per_col - 1:
            parts.append(jnp.zeros(((per_col - 1 - r) * DIFF_QK, bq), BF16))
        rhs = jnp.concatenate(parts, axis=0)
        dst_ref[:, g * bq:(g + 1) * bq] = jnp.dot(kj[:, col * LANES:(col + 1) * LANES], rhs,
                                                  preferred_element_type=F32)

    def step(src_ref, j, diag, dst_ref):
        vtj = vt_ref[:, pl.ds(pl.multiple_of(j * bk, bk), bk)]
        masked = diag is not None
        if masked:
            key = lax.broadcasted_iota(jnp.int32, (bk, bq), 0)
            qry = lax.broadcasted_iota(jnp.int32, (bk, bq), 1) + diag
            visible = key <= qry
        if dst_ref is not None:
            kn = key_tile(j + 1)
            for g in range(AHEAD):
                scores_into(dst_ref, kn, g)
        m_all = m_ref[...]
        new_m, new_acc = [], []
        for g in range(N_SM):
            s = src_ref[:, g * bq:(g + 1) * bq]
            if masked:
                s = jnp.where(visible, s, -jnp.inf)
            m_prev = m_all[:, g * bq:(g + 1) * bq]
            m_new = jnp.maximum(m_prev, jnp.max(s, axis=0, keepdims=True))
            alpha = jnp.exp2(m_prev - m_new)
            p = jnp.exp2(s - m_new).astype(BF16)
            if dst_ref is not None and g + AHEAD < N_SM:
                scores_into(dst_ref, kn, g + AHEAD)
            hd = g // 2
            lhs = jnp.concatenate([vtj[hd * DIFF_V:(hd + 1) * DIFF_V, :], ones], axis=0)
            pv = jnp.dot(lhs, p, preferred_element_type=F32)
            new_m.append(m_new)
            new_acc.append(acc_ref[g] * alpha + pv)
        m_ref[...] = jnp.concatenate(new_m, axis=1)
        for g in range(N_SM):
            acc_ref[g] = new_acc[g]

    k0 = key_tile(0)
    for g in range(N_SM):
        scores_into(sa_ref, k0, g)

    n = i // 2
    q_off = (i % 2) * bq

    def body(jj, carry):
        step(sa_ref, 2 * jj, None, sb_ref)
        step(sb_ref, 2 * jj + 1, None, sa_ref)
        return carry

    lax.fori_loop(0, n // 2, body, 0)

    @pl.when(n % 2 == 0)
    def _():
        step(sa_ref, n, q_off, None)

    @pl.when(n % 2 == 1)
    def _():
        step(sa_ref, n - 1, None, sb_ref)
        step(sb_ref, n, q_off, None)

    lam_p = lam_ref[...]
    s1 = jnp.sum(lam_p[0:1, :] * lam_p[1:2, :], axis=1, keepdims=True)
    s2 = jnp.sum(lam_p[2:3, :] * lam_p[3:4, :], axis=1, keepdims=True)
    lam = jnp.exp(s1) - jnp.exp(s2) + lambda_init
    gain = jnp.tile(gain_ref[...], (1, bq // LANES))
    outs = []
    for hd in range(DIFF_HEADS):
        a1 = acc_ref[2 * hd]
        a2 = acc_ref[2 * hd + 1]
        o1 = a1[0:DIFF_V, :] / a1[DIFF_V:DIFF_V + 1, :]
        o2 = a2[0:DIFF_V, :] / a2[DIFF_V:DIFF_V + 1, :]
        d = o1 - lam * o2
        ms = jnp.mean(d * d, axis=0, keepdims=True)
        outs.append(((d * lax.rsqrt(ms + EPS)) * gain) * (1.0 - lambda_init))
    o_ref[...] = jnp.concatenate(outs, axis=0).T.astype(o_ref.dtype)


def _diff_attn(qt, k, vt, lam_p, gain_b, lambda_init):
    B, S, W = k.shape
    grid = (B, S // BQ)
    return pl.pallas_call(
        functools.partial(_diff_attn_kernel, lambda_init=lambda_init),
        grid=grid,
        in_specs=[_const_spec(lam_p.shape), _const_spec(gain_b.shape),
                  pl.BlockSpec((None, W, BQ), lambda b, i: (b, 0, i)),
                  pl.BlockSpec((None, S, W), lambda b, i: (b, 0, 0)),
                  pl.BlockSpec((None, W, S), lambda b, i: (b, 0, 0))],
        out_specs=pl.BlockSpec((None, BQ, W), lambda b, i: (b, i, 0)),
        out_shape=jax.ShapeDtypeStruct((B, S, W), BF16),
        scratch_shapes=[pltpu.VMEM((1, N_SM * BQ), F32),
                        pltpu.VMEM((N_SM, ACC_ROWS, BQ), F32),
                        pltpu.VMEM((BK, N_SM * BQ), F32),
                        pltpu.VMEM((BK, N_SM * BQ), F32)],
        compiler_params=_cparams(("arbitrary", "arbitrary")),
        name="diff_attn",
    )(lam_p, gain_b, qt, k, vt)


N_PAIRS = DIL_W // LANES
DIL_UNROLL = 2


def _dil_block(qs, ks, vs, biases):
    nq = DIL_BLOCK
    low = lax.broadcasted_iota(jnp.int32, (nq, LANES), 1) < DIL_DH
    scores = []
    for qp, kp, bias in zip(qs, ks, biases):
        q2 = jnp.concatenate([jnp.where(low, qp, 0.0), jnp.where(low, 0.0, qp)], axis=0).astype(BF16)
        scores.append(lax.dot_general(q2, kp.astype(BF16), (((1,), (1,)), ((), ())),
                                      preferred_element_type=F32) + bias)
    out = []
    for s, vp in zip(scores, vs):
        m = jnp.max(s, axis=1, keepdims=True)
        p = jnp.exp(s - m)
        den = jnp.sum(p, axis=1, keepdims=True)
        num = jnp.dot(p.astype(BF16), vp.astype(BF16), preferred_element_type=F32)
        out.append(tuple(jnp.where(low, t[0:nq], t[nq:2 * nq]) for t in (num, den, m)))
    return out


def _dil_merge(a, b):
    (n1, d1, m1), (n2, d2, m2) = a, b
    m = jnp.maximum(m1, m2)
    w1 = jnp.exp(m1 - m)
    w2 = jnp.exp(m2 - m)
    return n1 * w1 + n2 * w2, d1 * w1 + d2 * w2, m


def _dil_attn_kernel(q_ref, k_ref, v_ref, o_ref, qf_ref, kk_ref, vv_ref, num_ref, den_ref, max_ref, bias_ref):
    c = pl.program_id(1)
    ch = q_ref.shape[0]

    iq = lax.broadcasted_iota(jnp.int32, (2 * DIL_BLOCK, 2 * DIL_BLOCK), 0) % DIL_BLOCK
    jk = lax.broadcasted_iota(jnp.int32, (2 * DIL_BLOCK, 2 * DIL_BLOCK), 1)
    band = (jk >= iq) & (jk <= iq + DIL_BLOCK)
    bias_ref[0] = jnp.where(band, 0.0, -jnp.inf)
    bias_ref[1] = jnp.where(band & (jk >= DIL_BLOCK), 0.0, -jnp.inf)

    @pl.when(c == 0)
    def _():
        kk_ref[:, 0:ch, :] = jnp.zeros((N_PAIRS, ch, LANES), F32)
        vv_ref[:, 0:ch, :] = jnp.zeros((N_PAIRS, ch, LANES), F32)

    for pr in range(N_PAIRS):
        sl = slice(pr * LANES, (pr + 1) * LANES)
        qf_ref[pr] = q_ref[:, sl].astype(F32)
        kk_ref[pr, ch:2 * ch, :] = k_ref[:, sl].astype(F32)
        vv_ref[pr, ch:2 * ch, :] = v_ref[:, sl].astype(F32)
    first_chunk = c == 0

    def blocks(specs):
        qs, ks, vs, biases = [], [], [], []
        for rows, keys, first_block in specs:
            bias = bias_ref[first_block.astype(jnp.int32)]
            for pr in range(N_PAIRS):
                qs.append(qf_ref[pr, rows, :])
                ks.append(kk_ref[pr, keys, :])
                vs.append(vv_ref[pr, keys, :])
                biases.append(bias)
        res = _dil_block(qs, ks, vs, biases)
        return [res[t * N_PAIRS:(t + 1) * N_PAIRS] for t in range(len(specs))]

    def merged(pr, rows, new):
        return _dil_merge((num_ref[pr, rows, :], den_ref[pr, rows, :], max_ref[pr, rows, :]), new)

    def store(pr, rows, ndm):
        num_ref[pr, rows, :] = ndm[0]
        den_ref[pr, rows, :] = ndm[1]
        max_ref[pr, rows, :] = ndm[2]

    d16 = DIL_PATTERNS[2][1]

    def body16(rr, carry):
        specs = [(pl.ds(DIL_UNROLL * rr + u, DIL_BLOCK, stride=d16),
                  pl.ds(DIL_UNROLL * rr + u, 2 * DIL_BLOCK, stride=d16), first_chunk) for u in range(DIL_UNROLL)]
        for (rows, _, _), res in zip(specs, blocks(specs)):
            for pr, new in enumerate(res):
                store(pr, rows, new)
        return carry

    lax.fori_loop(0, d16 // DIL_UNROLL, body16, 0)

    d4 = DIL_PATTERNS[1][1]
    blocks4 = ch // (d4 * DIL_BLOCK)

    def body4(tt, carry):
        nb = (DIL_UNROLL * tt) // d4
        first_block = jnp.logical_and(first_chunk, nb == 0)
        specs = []
        for u in range(DIL_UNROLL):
            q0 = (DIL_UNROLL * tt) % d4 + u + nb * (d4 * DIL_BLOCK)
            specs.append((pl.ds(q0, DIL_BLOCK, stride=d4),
                          pl.ds(ch + q0 - d4 * DIL_BLOCK, 2 * DIL_BLOCK, stride=d4), first_block))
        for (rows, _, _), res in zip(specs, blocks(specs)):
            for pr, new in enumerate(res):
                store(pr, rows, merged(pr, rows, new))
        return carry

    lax.fori_loop(0, d4 * blocks4 // DIL_UNROLL, body4, 0)

    def body1(nn, carry):
        specs = []
        for u in range(DIL_UNROLL):
            nb = DIL_UNROLL * nn + u
            q0 = pl.multiple_of(nb * DIL_BLOCK, DIL_BLOCK)
            specs.append((pl.ds(q0, DIL_BLOCK),
                          pl.ds(pl.multiple_of(ch + q0 - DIL_BLOCK, DIL_BLOCK), 2 * DIL_BLOCK),
                          jnp.logical_and(first_chunk, nb == 0)))
        for (rows, _, _), res in zip(specs, blocks(specs)):
            for pr, new in enumerate(res):
                n, d, _ = merged(pr, rows, new)
                o_ref[rows, pr * LANES:(pr + 1) * LANES] = (n / d).astype(o_ref.dtype)
        return carry

    lax.fori_loop(0, ch // DIL_BLOCK // DIL_UNROLL, body1, 0)

    kk_ref[:, 0:ch, :] = kk_ref[:, ch:2 * ch, :]
    vv_ref[:, 0:ch, :] = vv_ref[:, ch:2 * ch, :]


def _dil_attn(q, k, v):
    B, S, W = q.shape
    ch = DIL_CH
    blk = pl.BlockSpec((None, ch, W), lambda b, c: (b, c, 0))
    return pl.pallas_call(
        _dil_attn_kernel,
        grid=(B, S // ch),
        in_specs=[blk, blk, blk],
        out_specs=blk,
        out_shape=jax.ShapeDtypeStruct((B, S, W), BF16),
        scratch_shapes=[pltpu.VMEM((N_PAIRS, ch, LANES), F32), pltpu.VMEM((N_PAIRS, 2 * ch, LANES), F32),
                        pltpu.VMEM((N_PAIRS, 2 * ch, LANES), F32), pltpu.VMEM((N_PAIRS, ch, LANES), F32),
                        pltpu.VMEM((N_PAIRS, ch, LANES), F32), pltpu.VMEM((N_PAIRS, ch, LANES), F32),
                        pltpu.VMEM((2, 2 * DIL_BLOCK, 2 * DIL_BLOCK), F32)],
        compiler_params=_cparams(("arbitrary", "arbitrary")),
        name="dil_attn",
    )(q, k, v)


def _shift_rows(x, tail8, s):
    n = x.shape[0]
    row = lax.broadcasted_iota(jnp.int32, x.shape, 0)
    head = jnp.tile(pltpu.roll(tail8, s, 0), (n // SUBLANES, 1))
    return jnp.where(row < s, head, pltpu.roll(x, s, 0))


def _expand_heads(v, n_heads, width):
    rows = v.shape[0]
    lane = lax.broadcasted_iota(jnp.int32, (rows, n_heads * width), 1)
    out = jnp.zeros((rows, n_heads * width), F32)
    for h in range(n_heads):
        out = jnp.where(lane // width == h, v[:, h:h + 1], out)
    return out


def _ssd_kernel(z_ref, xbc_ref, dt_ref, cw_ref, cb_ref, dtb_ref, alog_ref, dvec_ref, gain_ref, tri_ref,
                o_ref, state_ref, tail_ref):
    c = pl.program_id(1)
    lc = xbc_ref.shape[0]
    L = SSD_CHUNK

    @pl.when(c == 0)
    def _():
        state_ref[...] = jnp.zeros(state_ref.shape, F32)
        tail_ref[...] = jnp.zeros(tail_ref.shape, F32)

    xbc = xbc_ref[...]
    tail8 = tail_ref[...]
    cw = cw_ref[...]
    acc = xbc * cw[SSD_CONV - 1:SSD_CONV, :] + cb_ref[...]
    for kk in range(SSD_CONV - 1):
        acc = acc + _shift_rows(xbc, tail8, SSD_CONV - 1 - kk) * cw[kk:kk + 1, :]
    tail_ref[...] = xbc[lc - SUBLANES:lc, :]
    xbc_c = _silu(acc)

    xs_all = xbc_c[:, 0:SSD_INNER]
    dt_raw = dt_ref[...] + dtb_ref[...]
    dt_all = jnp.maximum(dt_raw, 0.0) + jnp.log(1.0 + jnp.exp(-jnp.abs(dt_raw)))
    a_all = dt_all * (-jnp.exp(alog_ref[...]))
    tri = tri_ref[...]
    lane_in = lax.broadcasted_iota(jnp.int32, (L, SSD_INNER), 1)
    grp_w = SSD_INNER // SSD_GROUPS
    ii = lax.broadcasted_iota(jnp.int32, (L, L), 0)
    jj = lax.broadcasted_iota(jnp.int32, (L, L), 1)
    causal = jj <= ii

    for ck in range(lc // L):
        rs = slice(ck * L, (ck + 1) * L)
        xs = xs_all[rs, :]
        a = a_all[rs, :]
        hi, mid, lo = _split3(a)
        acs = (jnp.dot(tri, hi, preferred_element_type=F32) + jnp.dot(tri, mid, preferred_element_type=F32)
               + jnp.dot(tri, lo, preferred_element_type=F32))
        acs_t = acs.T
        dt_e = _expand_heads(dt_all[rs, :], SSD_HEADS, SSD_P)
        acs_e = _expand_heads(acs, SSD_HEADS, SSD_P)
        last_e = acs_e[L - 1:L, :]
        xdt = xs * dt_e
        xdt_b = xdt.astype(BF16)
        xdtd_b = (xdt * jnp.exp(last_e - acs_e)).astype(BF16)
        prev = state_ref[...]
        prev_b = prev.astype(BF16)
        y = xs * dvec_ref[...]
        new_state = prev * jnp.exp(last_e)
        y_off = jnp.zeros((L, SSD_INNER), F32)
        for g in range(SSD_GROUPS):
            bg = xbc_c[rs, SSD_INNER + g * SSD_N:SSD_INNER + (g + 1) * SSD_N]
            cg = xbc_c[rs, SSD_INNER + (SSD_GROUPS + g) * SSD_N:SSD_INNER + (SSD_GROUPS + g + 1) * SSD_N]
            bg_b = bg.astype(BF16)
            cg_b = cg.astype(BF16)
            in_grp = lane_in // grp_w == g
            cb = lax.dot_general(cg_b, bg_b, (((1,), (1,)), ((), ())), preferred_element_type=F32)
            for hh in range(SSD_HEADS // SSD_GROUPS):
                hd = g * (SSD_HEADS // SSD_GROUPS) + hh
                seg = acs[:, hd:hd + 1] - acs_t[hd:hd + 1, :]
                w = jnp.where(causal, cb * jnp.exp(seg), 0.0).astype(BF16)
                y = y + jnp.dot(w, jnp.where(lane_in // SSD_P == hd, xdt_b, jnp.zeros_like(xdt_b)),
                                preferred_element_type=F32)
            zero_b = jnp.zeros_like(xdtd_b)
            new_state = new_state + jnp.dot(bg.T.astype(BF16), jnp.where(in_grp, xdtd_b, zero_b),
                                            preferred_element_type=F32)
            y_off = y_off + jnp.dot(cg_b, jnp.where(in_grp, prev_b, jnp.zeros_like(prev_b)),
                                    preferred_element_type=F32)
        state_ref[...] = new_state
        y = y + y_off * jnp.exp(acs_e)
        y = y * _silu(z_ref[rs, :].astype(F32))
        ysq = y * y
        in0 = lane_in < grp_w
        ms0 = jnp.sum(jnp.where(in0, ysq, 0.0), axis=1, keepdims=True) * (1.0 / grp_w)
        ms1 = jnp.sum(jnp.where(in0, 0.0, ysq), axis=1, keepdims=True) * (1.0 / grp_w)
        r = jnp.where(in0, lax.rsqrt(ms0 + EPS), lax.rsqrt(ms1 + EPS))
        o_ref[rs, :] = ((y * r) * gain_ref[...]).astype(o_ref.dtype)


def _ssd(z, xbc, dt, cw, cb, dtb, alog, dvec, gain, tri):
    B, S, _ = z.shape
    lc = SSD_LC
    tok = lambda w: pl.BlockSpec((None, lc, w), lambda b, c: (b, c, 0))
    consts = (cw, cb, dtb, alog, dvec, gain, tri)
    return pl.pallas_call(
        _ssd_kernel,
        grid=(B, S // lc),
        in_specs=[tok(SSD_INNER), tok(SSD_XBC), tok(LANES)] + [_const_spec(a.shape) for a in consts],
        out_specs=tok(SSD_INNER),
        out_shape=jax.ShapeDtypeStruct((B, S, SSD_INNER), BF16),
        scratch_shapes=[pltpu.VMEM((SSD_N, SSD_INNER), F32), pltpu.VMEM((SUBLANES, SSD_XBC), F32)],
        compiler_params=_cparams(("arbitrary", "arbitrary")),
        name="ssd",
    )(z, xbc, dt, *consts)


def _rms(x, gain):
    ms = jnp.mean(x * x, axis=-1, keepdims=True)
    return (x * lax.rsqrt(ms + EPS)) * gain


def _out_ffn_kernel(x_ref, od_ref, ol_ref, os_ref, wo_d_ref, wo_l_ref, wo_s_ref, g_mix_ref, g_pre_ref,
                    up_ref, cw_ref, cb_ref, down_ref, g_post_ref, o_ref, tail_ref, *, tiles_per_seq):
    t = pl.program_id(0)
    tm = x_ref.shape[0]

    @pl.when(t % tiles_per_seq == 0)
    def _():
        tail_ref[...] = jnp.zeros(tail_ref.shape, F32)

    mix = (jnp.dot(od_ref[...], wo_d_ref[...], preferred_element_type=F32)
           + jnp.dot(ol_ref[...], wo_l_ref[...], preferred_element_type=F32)
           + jnp.dot(os_ref[...], wo_s_ref[...], preferred_element_type=F32))
    x1 = x_ref[...] + _rms(mix, g_mix_ref[...])
    h = _rms(x1, g_pre_ref[...]).astype(BF16)

    def up(ci):
        return [jnp.dot(h, up_ref[:, off + ci * FF_CHUNK:off + (ci + 1) * FF_CHUNK], preferred_element_type=F32)
                for off in (0, D_FF)]

    n_chunks = D_FF // FF_CHUNK
    f_acc = jnp.zeros((tm, D_MODEL), F32)
    pending = [up(ci) for ci in range(FF_AHEAD)]
    for ci in range(n_chunks):
        u_pair = pending.pop(0)
        if ci + FF_AHEAD < n_chunks:
            pending.append(up(ci + FF_AHEAD))
        halves = []
        for u, off in zip(u_pair, (0, D_FF)):
            cs = slice(off + ci * FF_CHUNK, off + (ci + 1) * FF_CHUNK)
            tail8 = tail_ref[:, cs]
            cw = cw_ref[:, cs]
            conv = u * cw[FFN_CONV - 1:FFN_CONV, :] + cb_ref[:, cs]
            for kk in range(FFN_CONV - 1):
                conv = conv + _shift_rows(u, tail8, FFN_CONV - 1 - kk) * cw[kk:kk + 1, :]
            tail_ref[:, cs] = u[tm - SUBLANES:tm, :]
            halves.append(conv)
        f = (_silu(halves[0]) * halves[1]).astype(BF16)
        f_acc = f_acc + jnp.dot(f, down_ref[ci * FF_CHUNK:(ci + 1) * FF_CHUNK, :], preferred_element_type=F32)
    o_ref[...] = x1 + _rms(f_acc, g_post_ref[...])


def _out_ffn(x2d, od, ol, os_, wo_d, wo_l, wo_s, g_mix, g_pre, up, cw, cb, down, g_post, tiles_per_seq):
    T, D = x2d.shape
    tm = TM_FFN
    tok = lambda w: pl.BlockSpec((tm, w), lambda t: (t, 0))
    consts = (wo_d, wo_l, wo_s, g_mix, g_pre, up, cw, cb, down, g_post)
    return pl.pallas_call(
        functools.partial(_out_ffn_kernel, tiles_per_seq=tiles_per_seq),
        grid=(T // tm,),
        in_specs=[tok(D), tok(DIFF_W), tok(DIL_W), tok(SSD_INNER)]
                 + [pl.BlockSpec(a.shape, lambda t: (0, 0), pipeline_mode=pl.Buffered(1)) for a in consts],
        out_specs=tok(D),
        out_shape=jax.ShapeDtypeStruct((T, D), F32),
        scratch_shapes=[pltpu.VMEM((SUBLANES, 2 * D_FF), F32)],
        compiler_params=_cparams(("arbitrary",)),
        name="out_ffn",
    )(x2d, od, ol, os_, *consts)


def _rope_tables(S, half):
    inv_freq = jnp.exp(-math.log(ROPE_THETA) * jnp.arange(half, dtype=F32) / half)
    ang = jnp.arange(S, dtype=F32)[:, None] * inv_freq[None, :]
    cos, sin = jnp.cos(ang), jnp.sin(ang)
    zero = jnp.zeros_like(sin)
    reps = LANES // (2 * half)
    cos_t = jnp.tile(jnp.concatenate([cos, cos], axis=1), (1, reps))
    sin_up = jnp.tile(jnp.concatenate([-sin, zero], axis=1), (1, reps))
    sin_dn = jnp.tile(jnp.concatenate([zero, sin], axis=1), (1, reps))
    return cos_t, sin_up, sin_dn


def _rope_tables_t(S, half):
    inv_freq = jnp.exp(-math.log(ROPE_THETA) * jnp.arange(half, dtype=F32) / half)
    ang = jnp.arange(S, dtype=F32)[:, None] * inv_freq[None, :]
    return jnp.cos(ang).T, jnp.sin(ang).T


def kernel(x, pre_mix_norm, w_in, diff_lambda, diff_head_norm, ssd_conv_w, ssd_conv_b, ssd_dt_bias, ssd_A_log,
           ssd_D, ssd_norm, w_out, post_mix_norm, pre_ffn_norm, ffn_up, ffn_conv_w, ffn_conv_b, ffn_down,
           post_ffn_norm):
    B, S, D = x.shape
    depth = w_in.shape[0]
    assert D == D_MODEL and S % DIL_CH == 0 and S % TM_IN == 0 and S % BQ == 0 and S % TM_FFN == 0
    tables = _rope_tables_t(S, DIFF_QK // 2) + _rope_tables(S, DIFF_QK // 2) + _rope_tables(S, DIL_DH // 2)
    tri = (lax.broadcasted_iota(jnp.int32, (SSD_CHUNK, SSD_CHUNK), 1)
           <= lax.broadcasted_iota(jnp.int32, (SSD_CHUNK, SSD_CHUNK), 0)).astype(BF16)
    offs = [0]
    for sz in IN_SIZES:
        offs.append(offs[-1] + sz)

    def pad_lanes(v):
        return jnp.pad(v, ((0, 0), (0, LANES - v.shape[1])))

    for layer in range(depth):
        lambda_init = 0.8 - 0.6 * math.exp(-0.3 * layer)
        wl = w_in[layer].astype(BF16)
        w_a = jnp.concatenate([wl[:, offs[1]:offs[2]], wl[:, offs[3]:offs[5]]], axis=1)
        w_b = jnp.pad(wl[:, offs[5]:], ((0, 0), (0, LANES - SSD_HEADS)))
        w_t = jnp.concatenate([wl[:, offs[0]:offs[1]], wl[:, offs[2]:offs[3]]], axis=1).T
        ws = [w_t, w_a, w_b]
        dqt, dk, dvt, lq, lk, lv, z, xbc, dt = _in_proj(x, pre_mix_norm[layer][None, :], tables, ws)

        o_diff = _diff_attn(dqt, dk, dvt, diff_lambda[layer],
                            jnp.broadcast_to(diff_head_norm[layer][:, None], (DIFF_V, LANES)), lambda_init)
        o_dil = _dil_attn(lq, lk, lv)
        o_ssd = _ssd(z, xbc, dt, ssd_conv_w[layer], ssd_conv_b[layer][None, :],
                     pad_lanes(ssd_dt_bias[layer][None, :]), pad_lanes(ssd_A_log[layer][None, :]),
                     jnp.repeat(ssd_D[layer], SSD_P)[None, :], ssd_norm[layer][None, :], tri)

        wo = w_out[layer].astype(BF16)
        x2d = _out_ffn(
            x.reshape(B * S, D), o_diff.reshape(B * S, DIFF_W), o_dil.reshape(B * S, DIL_W),
            o_ssd.reshape(B * S, SSD_INNER),
            wo[0:DIFF_W], wo[DIFF_W:DIFF_W + DIL_W], wo[DIFF_W + DIL_W:],
            post_mix_norm[layer][None, :], pre_ffn_norm[layer][None, :],
            ffn_up[layer].astype(BF16), ffn_conv_w[layer], ffn_conv_b[layer][None, :],
            ffn_down[layer].astype(BF16), post_ffn_norm[layer][None, :], S // TM_FFN)
        x = x2d.reshape(B, S, D)
    return x
```

```python
import functools
import math

import jax
import jax.numpy as jnp
from jax import lax
from jax.experimental import pallas as pl
from jax.experimental.pallas import tpu as pltpu

F32 = jnp.float32
BF16 = jnp.bfloat16

LANES = 128
SUBLANES = 8
VMEM_LIMIT = 56 * 1024 * 1024

D_MODEL = 1024
DIFF_HEADS = 4
DIFF_QK = 32
DIFF_V = 64
DIFF_W = DIFF_HEADS * DIFF_V
DIL_HEADS = 6
DIL_DH = 64
DIL_W = DIL_HEADS * DIL_DH
DIL_PATTERNS = ((128, 1), (512, 4), (2048, 16))
DIL_BLOCK = 128
SSD_HEADS = 6
SSD_P = 64
SSD_GROUPS = 2
SSD_N = 128
SSD_CONV = 4
SSD_CHUNK = 128
SSD_INNER = SSD_HEADS * SSD_P
SSD_XBC = SSD_INNER + 2 * SSD_GROUPS * SSD_N
D_FF = 2816
FFN_CONV = 3
ROPE_THETA = 10000.0
EPS = 1e-6
LOG2E = 1.4426950408889634

IN_SIZES = (DIFF_W, DIFF_W, DIFF_W, DIL_W, DIL_W, DIL_W, SSD_INNER, SSD_XBC, SSD_HEADS)

TM_IN = 512
BQ = 256
BK = 2 * BQ
DIL_CH = 2048
SSD_LC = 512
TM_FFN = 256
FF_CHUNK = 256
FF_AHEAD = 2


def _cparams(sem):
    return pltpu.CompilerParams(dimension_semantics=sem, vmem_limit_bytes=VMEM_LIMIT)


def _const_spec(shape):
    nd = len(shape)
    return pl.BlockSpec(shape, lambda *_: (0,) * nd)


def _silu(x):
    return x * (1.0 / (1.0 + jnp.exp(-x)))


def _split3(x):
    hi = x.astype(BF16)
    r = x - hi.astype(F32)
    mid = r.astype(BF16)
    lo = (r - mid.astype(F32)).astype(BF16)
    return hi, mid, lo


def _rope_lanes(y, cos, sin_up, sin_dn, half):
    outs = []
    for j in range(y.shape[1] // LANES):
        yc = y[:, j * LANES:(j + 1) * LANES]
        up = pltpu.roll(yc, LANES - half, 1)
        dn = pltpu.roll(yc, half, 1)
        outs.append(yc * cos + up * sin_up + dn * sin_dn)
    return jnp.concatenate(outs, axis=1)


def _rope_rows(yt, cos_t, sin_t, half):
    outs = []
    for g in range(yt.shape[0] // (2 * half)):
        x1 = yt[2 * half * g:2 * half * g + half, :]
        x2 = yt[2 * half * g + half:2 * half * (g + 1), :]
        outs += [x1 * cos_t - x2 * sin_t, x2 * cos_t + x1 * sin_t]
    return jnp.concatenate(outs, axis=0)


def _in_proj_kernel(x_ref, g_ref, cdt_ref, sdt_ref, cd_ref, sud_ref, sdd_ref, cl_ref, sul_ref, sdl_ref,
                    wt_ref, wa_ref, wb_ref,
                    oqt_ref, ok_ref, ovt_ref, olq_ref, olk_ref, olv_ref, oz_ref, oxbc_ref, odt_ref):
    x = x_ref[...]
    ms = jnp.mean(x * x, axis=-1, keepdims=True)
    h = ((x * lax.rsqrt(ms + EPS)) * g_ref[...]).astype(BF16)

    def proj(w_ref):
        return jnp.dot(h, w_ref[...], preferred_element_type=F32)

    def proj_t(wt_ref):
        return lax.dot_general(wt_ref[...], h, (((1,), (1,)), ((), ())), preferred_element_type=F32)

    r_t = proj_t(wt_ref)
    r_qt, r_vt = r_t[0:DIFF_W, :], r_t[DIFF_W:2 * DIFF_W, :]
    r_a = proj(wa_ref)
    r_b = proj(wb_ref)

    qt = _rope_rows(r_qt, cdt_ref[...], sdt_ref[...], DIFF_QK // 2)
    oqt_ref[...] = (qt * (DIFF_QK ** -0.5 * LOG2E)).astype(BF16)
    cd, sud, sdd = cd_ref[...], sud_ref[...], sdd_ref[...]
    a0, a1, a2 = DIFF_W, DIFF_W + DIL_W, DIFF_W + 2 * DIL_W
    ok_ref[...] = _rope_lanes(r_a[:, 0:a0], cd, sud, sdd, DIFF_QK // 2).astype(BF16)
    cl, sul, sdl = cl_ref[...], sul_ref[...], sdl_ref[...]
    lq = _rope_lanes(r_a[:, a0:a1], cl, sul, sdl, DIL_DH // 2)
    olq_ref[...] = (lq * (DIL_DH ** -0.5)).astype(BF16)
    olk_ref[...] = _rope_lanes(r_a[:, a1:a2], cl, sul, sdl, DIL_DH // 2).astype(BF16)
    b0, b1, b2 = DIL_W, DIL_W + SSD_INNER, DIL_W + SSD_INNER + SSD_XBC
    olv_ref[...] = r_b[:, 0:b0].astype(BF16)
    oz_ref[...] = r_b[:, b0:b1].astype(BF16)
    oxbc_ref[...] = r_b[:, b1:b2]
    odt_ref[...] = r_b[:, b2:b2 + LANES]
    ovt_ref[...] = r_vt.astype(BF16)


def _in_proj(x, gain, tables, ws):
    B, S, D = x.shape
    tm = TM_IN
    grid = (S // tm, B)
    tok = lambda w: pl.BlockSpec((None, tm, w), lambda s, b: (b, s, 0))
    tok_t = lambda w: pl.BlockSpec((None, w, tm), lambda s, b: (b, 0, s))
    tab = pl.BlockSpec((tm, LANES), lambda s, b: (s, 0))
    tab_t = pl.BlockSpec((DIFF_QK // 2, tm), lambda s, b: (0, s))
    out_widths = (DIFF_W, DIFF_W, DIFF_W, DIL_W, DIL_W, DIL_W, SSD_INNER, SSD_XBC, LANES)
    out_dtypes = (BF16, BF16, BF16, BF16, BF16, BF16, BF16, F32, F32)
    transposed = (True, False, True) + (False,) * 6
    return pl.pallas_call(
        _in_proj_kernel,
        grid=grid,
        in_specs=[tok(D), _const_spec((1, D))] + [tab_t] * 2 + [tab] * 6 + [_const_spec(w.shape) for w in ws],
        out_specs=[tok_t(w) if t else tok(w) for w, t in zip(out_widths, transposed)],
        out_shape=[jax.ShapeDtypeStruct((B, w, S) if t else (B, S, w), dt)
                   for w, dt, t in zip(out_widths, out_dtypes, transposed)],
        compiler_params=_cparams(("arbitrary", "arbitrary")),
        name="in_proj",
    )(x, gain, *tables, *ws)


N_SM = 2 * DIFF_HEADS
ACC_ROWS = DIFF_V + 16
AHEAD = 2


def _diff_attn_kernel(lam_ref, gain_ref, qt_ref, k_ref, vt_ref, o_ref, m_ref, acc_ref, sa_ref, sb_ref, *,
                      lambda_init):
    i = pl.program_id(1)
    bq = qt_ref.shape[1]
    m_ref[...] = jnp.full(m_ref.shape, -jnp.inf, F32)
    acc_ref[...] = jnp.zeros(acc_ref.shape, F32)
    bk = sa_ref.shape[0]
    ones = jnp.ones((ACC_ROWS - DIFF_V, bk), BF16)

    def key_tile(j):
        return k_ref[pl.ds(pl.multiple_of(j * bk, bk), bk), :]

    def scores_into(dst_ref, kj, g):
        per_col = LANES // DIFF_QK
        col, r = g // per_col, g % per_col
        parts = [qt_ref[g * DIFF_QK:(g + 1) * DIFF_QK, :]]
        if r > 0:
            parts.insert(0, jnp.zeros((r * DIFF_QK, bq), BF16))
        if r < per_col - 1:
            parts.append(jnp.zeros(((per_col - 1 - r) * DIFF_QK, bq), BF16))
        rhs = jnp.concatenate(parts, axis=0)
        dst_ref[:, g * bq:(g + 1) * bq] = jnp.dot(kj[:, col * LANES:(col + 1) * LANES], rhs,
                                                  preferred_element_type=F32)

    def step(src_ref, j, diag, dst_ref):
        vtj = vt_ref[:, pl.ds(pl.multiple_of(j * bk, bk), bk)]
        masked = diag is not None
        if masked:
            key = lax.broadcasted_iota(jnp.int32, (bk, bq), 0)
            qry = lax.broadcasted_iota(jnp.int32, (bk, bq), 1) + diag
            visible = key <= qry
        if dst_ref is not None:
            kn = key_tile(j + 1)
            for g in range(AHEAD):
                scores_into(dst_ref, kn, g)
        m_all = m_ref[...]
        new_m, new_acc = [], []
        for g in range(N_SM):
            s = src_ref[:, g * bq:(g + 1) * bq]
            if masked:
                s = jnp.where(visible, s, -jnp.inf)
            m_prev = m_all[:, g * bq:(g + 1) * bq]
            m_new = jnp.maximum(m_prev, jnp.max(s, axis=0, keepdims=True))
            alpha = jnp.exp2(m_prev - m_new)
            p = jnp.exp2(s - m_new).astype(BF16)
            if dst_ref is not None and g + AHEAD < N_SM:
                scores_into(dst_ref, kn, g + AHEAD)
            hd = g // 2
            lhs = jnp.concatenate([vtj[hd * DIFF_V:(hd + 1) * DIFF_V, :], ones], axis=0)
            pv = jnp.dot(lhs, p, preferred_element_type=F32)
            new_m.append(m_new)
            new_acc.append(acc_ref[g] * alpha + pv)
        m_ref[...] = jnp.concatenate(new_m, axis=1)
        for g in range(N_SM):
            acc_ref[g] = new_acc[g]

    k0 = key_tile(0)
    for g in range(N_SM):
        scores_into(sa_ref, k0, g)

    n = i // 2
    q_off = (i % 2) * bq

    def body(jj, carry):
        step(sa_ref, 2 * jj, None, sb_ref)
        step(sb_ref, 2 * jj + 1, None, sa_ref)
        return carry

    lax.fori_loop(0, n // 2, body, 0)

    @pl.when(n % 2 == 0)
    def _():
        step(sa_ref, n, q_off, None)

    @pl.when(n % 2 == 1)
    def _():
        step(sa_ref, n - 1, None, sb_ref)
        step(sb_ref, n, q_off, None)

    lam_p = lam_ref[...]
    s1 = jnp.sum(lam_p[0:1, :] * lam_p[1:2, :], axis=1, keepdims=True)
    s2 = jnp.sum(lam_p[2:3, :] * lam_p[3:4, :], axis=1, keepdims=True)
    lam = jnp.exp(s1) - jnp.exp(s2) + lambda_init
    gain = jnp.tile(gain_ref[...], (1, bq // LANES))
    outs = []
    for hd in range(DIFF_HEADS):
        a1 = acc_ref[2 * hd]
        a2 = acc_ref[2 * hd + 1]
        o1 = a1[0:DIFF_V, :] / a1[DIFF_V:DIFF_V + 1, :]
        o2 = a2[0:DIFF_V, :] / a2[DIFF_V:DIFF_V + 1, :]
        d = o1 - lam * o2
        ms = jnp.mean(d * d, axis=0, keepdims=True)
        outs.append(((d * lax.rsqrt(ms + EPS)) * gain) * (1.0 - lambda_init))
    o_ref[...] = jnp.concatenate(outs, axis=0).T.astype(o_ref.dtype)


def _diff_attn(qt, k, vt, lam_p, gain_b, lambda_init):
    B, S, W = k.shape
    grid = (B, S // BQ)
    return pl.pallas_call(
        functools.partial(_diff_attn_kernel, lambda_init=lambda_init),
        grid=grid,
        in_specs=[_const_spec(lam_p.shape), _const_spec(gain_b.shape),
                  pl.BlockSpec((None, W, BQ), lambda b, i: (b, 0, i)),
                  pl.BlockSpec((None, S, W), lambda b, i: (b, 0, 0)),
                  pl.BlockSpec((None, W, S), lambda b, i: (b, 0, 0))],
        out_specs=pl.BlockSpec((None, BQ, W), lambda b, i: (b, i, 0)),
        out_shape=jax.ShapeDtypeStruct((B, S, W), BF16),
        scratch_shapes=[pltpu.VMEM((1, N_SM * BQ), F32),
                        pltpu.VMEM((N_SM, ACC_ROWS, BQ), F32),
                        pltpu.VMEM((BK, N_SM * BQ), F32),
                        pltpu.VMEM((BK, N_SM * BQ), F32)],
        compiler_params=_cparams(("arbitrary", "arbitrary")),
        name="diff_attn",
    )(lam_p, gain_b, qt, k, vt)


N_PAIRS = DIL_W // LANES
DIL_UNROLL = 4


def _dil_block(qs, ks, vs, biases):
    nq = DIL_BLOCK
    low = lax.broadcasted_iota(jnp.int32, (nq, LANES), 1) < DIL_DH
    scores = []
    for qp, kp, bias in zip(qs, ks, biases):
        q2 = jnp.concatenate([jnp.where(low, qp, 0.0), jnp.where(low, 0.0, qp)], axis=0).astype(BF16)
        scores.append(lax.dot_general(q2, kp.astype(BF16), (((1,), (1,)), ((), ())),
                                      preferred_element_type=F32) + bias)
    out = []
    for s, vp in zip(scores, vs):
        m = jnp.max(s, axis=1, keepdims=True)
        p = jnp.exp(s - m)
        den = jnp.sum(p, axis=1, keepdims=True)
        num = jnp.dot(p.astype(BF16), vp.astype(BF16), preferred_element_type=F32)
        out.append(tuple(jnp.where(low, t[0:nq], t[nq:2 * nq]) for t in (num, den, m)))
    return out


def _dil_merge(a, b):
    (n1, d1, m1), (n2, d2, m2) = a, b
    m = jnp.maximum(m1, m2)
    w1 = jnp.exp(m1 - m)
    w2 = jnp.exp(m2 - m)
    return n1 * w1 + n2 * w2, d1 * w1 + d2 * w2, m


def _dil_attn_kernel(q_ref, k_ref, v_ref, o_ref, qf_ref, kk_ref, vv_ref, num_ref, den_ref, max_ref, bias_ref):
    c = pl.program_id(1)
    ch = q_ref.shape[0]

    iq = lax.broadcasted_iota(jnp.int32, (2 * DIL_BLOCK, 2 * DIL_BLOCK), 0) % DIL_BLOCK
    jk = lax.broadcasted_iota(jnp.int32, (2 * DIL_BLOCK, 2 * DIL_BLOCK), 1)
    band = (jk >= iq) & (jk <= iq + DIL_BLOCK)
    bias_ref[0] = jnp.where(band, 0.0, -jnp.inf)
    bias_ref[1] = jnp.where(band & (jk >= DIL_BLOCK), 0.0, -jnp.inf)

    @pl.when(c == 0)
    def _():
        kk_ref[:, 0:ch, :] = jnp.zeros((N_PAIRS, ch, LANES), F32)
        vv_ref[:, 0:ch, :] = jnp.zeros((N_PAIRS, ch, LANES), F32)

    for pr in range(N_PAIRS):
        sl = slice(pr * LANES, (pr + 1) * LANES)
        qf_ref[pr] = q_ref[:, sl].astype(F32)
        kk_ref[pr, ch:2 * ch, :] = k_ref[:, sl].astype(F32)
        vv_ref[pr, ch:2 * ch, :] = v_ref[:, sl].astype(F32)
    first_chunk = c == 0

    def blocks(specs):
        qs, ks, vs, biases = [], [], [], []
        for rows, keys, first_block in specs:
            bias = bias_ref[first_block.astype(jnp.int32)]
            for pr in range(N_PAIRS):
                qs.append(qf_ref[pr, rows, :])
                ks.append(kk_ref[pr, keys, :])
                vs.append(vv_ref[pr, keys, :])
                biases.append(bias)
        res = _dil_block(qs, ks, vs, biases)
        return [res[t * N_PAIRS:(t + 1) * N_PAIRS] for t in range(len(specs))]

    def merged(pr, rows, new):
        return _dil_merge((num_ref[pr, rows, :], den_ref[pr, rows, :], max_ref[pr, rows, :]), new)

    def store(pr, rows, ndm):
        num_ref[pr, rows, :] = ndm[0]
        den_ref[pr, rows, :] = ndm[1]
        max_ref[pr, rows, :] = ndm[2]

    d16 = DIL_PATTERNS[2][1]

    def body16(rr, carry):
        specs = [(pl.ds(DIL_UNROLL * rr + u, DIL_BLOCK, stride=d16),
                  pl.ds(DIL_UNROLL * rr + u, 2 * DIL_BLOCK, stride=d16), first_chunk) for u in range(DIL_UNROLL)]
        for (rows, _, _), res in zip(specs, blocks(specs)):
            for pr, new in enumerate(res):
                store(pr, rows, new)
        return carry

    lax.fori_loop(0, d16 // DIL_UNROLL, body16, 0)

    d4 = DIL_PATTERNS[1][1]
    blocks4 = ch // (d4 * DIL_BLOCK)

    def body4(tt, carry):
        nb = (DIL_UNROLL * tt) // d4
        first_block = jnp.logical_and(first_chunk, nb == 0)
        specs = []
        for u in range(DIL_UNROLL):
            q0 = (DIL_UNROLL * tt) % d4 + u + nb * (d4 * DIL_BLOCK)
            specs.append((pl.ds(q0, DIL_BLOCK, stride=d4),
                          pl.ds(ch + q0 - d4 * DIL_BLOCK, 2 * DIL_BLOCK, stride=d4), first_block))
        for (rows, _, _), res in zip(specs, blocks(specs)):
            for pr, new in enumerate(res):
                store(pr, rows, merged(pr, rows, new))
        return carry

    lax.fori_loop(0, d4 * blocks4 // DIL_UNROLL, body4, 0)

    def body1(nn, carry):
        specs = []
        for u in range(DIL_UNROLL):
            nb = DIL_UNROLL * nn + u
            q0 = pl.multiple_of(nb * DIL_BLOCK, DIL_BLOCK)
            specs.append((pl.ds(q0, DIL_BLOCK),
                          pl.ds(pl.multiple_of(ch + q0 - DIL_BLOCK, DIL_BLOCK), 2 * DIL_BLOCK),
                          jnp.logical_and(first_chunk, nb == 0)))
        for (rows, _, _), res in zip(specs, blocks(specs)):
            for pr, new in enumerate(res):
                n, d, _ = merged(pr, rows, new)
                o_ref[rows, pr * LANES:(pr + 1) * LANES] = (n / d).astype(o_ref.dtype)
        return carry

    lax.fori_loop(0, ch // DIL_BLOCK // DIL_UNROLL, body1, 0)

    kk_ref[:, 0:ch, :] = kk_ref[:, ch:2 * ch, :]
    vv_ref[:, 0:ch, :] = vv_ref[:, ch:2 * ch, :]


def _dil_attn(q, k, v):
    B, S, W = q.shape
    ch = DIL_CH
    blk = pl.BlockSpec((None, ch, W), lambda b, c: (b, c, 0))
    return pl.pallas_call(
        _dil_attn_kernel,
        grid=(B, S // ch),
        in_specs=[blk, blk, blk],
        out_specs=blk,
        out_shape=jax.ShapeDtypeStruct((B, S, W), BF16),
        scratch_shapes=[pltpu.VMEM((N_PAIRS, ch, LANES), F32), pltpu.VMEM((N_PAIRS, 2 * ch, LANES), F32),
                        pltpu.VMEM((N_PAIRS, 2 * ch, LANES), F32), pltpu.VMEM((N_PAIRS, ch, LANES), F32),
                        pltpu.VMEM((N_PAIRS, ch, LANES), F32), pltpu.VMEM((N_PAIRS, ch, LANES), F32),
                        pltpu.VMEM((2, 2 * DIL_BLOCK, 2 * DIL_BLOCK), F32)],
        compiler_params=_cparams(("arbitrary", "arbitrary")),
        name="dil_attn",
    )(q, k, v)


def _shift_rows(x, tail8, s):
    n = x.shape[0]
    row = lax.broadcasted_iota(jnp.int32, x.shape, 0)
    head = jnp.tile(pltpu.roll(tail8, s, 0), (n // SUBLANES, 1))
    return jnp.where(row < s, head, pltpu.roll(x, s, 0))


def _expand_heads(v, n_heads, width):
    rows = v.shape[0]
    lane = lax.broadcasted_iota(jnp.int32, (rows, n_heads * width), 1)
    out = jnp.zeros((rows, n_heads * width), F32)
    for h in range(n_heads):
        out = jnp.where(lane // width == h, v[:, h:h + 1], out)
    return out


def _ssd_kernel(z_ref, xbc_ref, dt_ref, cw_ref, cb_ref, dtb_ref, alog_ref, dvec_ref, gain_ref, tri_ref,
                o_ref, state_ref, tail_ref):
    c = pl.program_id(1)
    lc = xbc_ref.shape[0]
    L = SSD_CHUNK

    @pl.when(c == 0)
    def _():
        state_ref[...] = jnp.zeros(state_ref.shape, F32)
        tail_ref[...] = jnp.zeros(tail_ref.shape, F32)

    xbc = xbc_ref[...]
    tail8 = tail_ref[...]
    cw = cw_ref[...]
    acc = xbc * cw[SSD_CONV - 1:SSD_CONV, :] + cb_ref[...]
    for kk in range(SSD_CONV - 1):
        acc = acc + _shift_rows(xbc, tail8, SSD_CONV - 1 - kk) * cw[kk:kk + 1, :]
    tail_ref[...] = xbc[lc - SUBLANES:lc, :]
    xbc_c = _silu(acc)

    xs_all = xbc_c[:, 0:SSD_INNER]
    dt_raw = dt_ref[...] + dtb_ref[...]
    dt_all = jnp.maximum(dt_raw, 0.0) + jnp.log(1.0 + jnp.exp(-jnp.abs(dt_raw)))
    a_all = dt_all * (-jnp.exp(alog_ref[...]))
    tri = tri_ref[...]
    lane_in = lax.broadcasted_iota(jnp.int32, (L, SSD_INNER), 1)
    grp_w = SSD_INNER // SSD_GROUPS
    ii = lax.broadcasted_iota(jnp.int32, (L, L), 0)
    jj = lax.broadcasted_iota(jnp.int32, (L, L), 1)
    causal = jj <= ii

    for ck in range(lc // L):
        rs = slice(ck * L, (ck + 1) * L)
        xs = xs_all[rs, :]
        a = a_all[rs, :]
        hi, mid, lo = _split3(a)
        acs = (jnp.dot(tri, hi, preferred_element_type=F32) + jnp.dot(tri, mid, preferred_element_type=F32)
               + jnp.dot(tri, lo, preferred_element_type=F32))
        acs_t = acs.T
        dt_e = _expand_heads(dt_all[rs, :], SSD_HEADS, SSD_P)
        acs_e = _expand_heads(acs, SSD_HEADS, SSD_P)
        last_e = acs_e[L - 1:L, :]
        xdt = xs * dt_e
        xdt_b = xdt.astype(BF16)
        xdtd_b = (xdt * jnp.exp(last_e - acs_e)).astype(BF16)
        prev = state_ref[...]
        prev_b = prev.astype(BF16)
        y = xs * dvec_ref[...]
        new_state = prev * jnp.exp(last_e)
        y_off = jnp.zeros((L, SSD_INNER), F32)
        for g in range(SSD_GROUPS):
            bg = xbc_c[rs, SSD_INNER + g * SSD_N:SSD_INNER + (g + 1) * SSD_N]
            cg = xbc_c[rs, SSD_INNER + (SSD_GROUPS + g) * SSD_N:SSD_INNER + (SSD_GROUPS + g + 1) * SSD_N]
            bg_b = bg.astype(BF16)
            cg_b = cg.astype(BF16)
            in_grp = lane_in // grp_w == g
            cb = lax.dot_general(cg_b, bg_b, (((1,), (1,)), ((), ())), preferred_element_type=F32)
            for hh in range(SSD_HEADS // SSD_GROUPS):
                hd = g * (SSD_HEADS // SSD_GROUPS) + hh
                seg = acs[:, hd:hd + 1] - acs_t[hd:hd + 1, :]
                w = jnp.where(causal, cb * jnp.exp(seg), 0.0).astype(BF16)
                y = y + jnp.dot(w, jnp.where(lane_in // SSD_P == hd, xdt_b, jnp.zeros_like(xdt_b)),
                                preferred_element_type=F32)
            zero_b = jnp.zeros_like(xdtd_b)
            new_state = new_state + jnp.dot(bg.T.astype(BF16), jnp.where(in_grp, xdtd_b, zero_b),
                                            preferred_element_type=F32)
            y_off = y_off + jnp.dot(cg_b, jnp.where(in_grp, prev_b, jnp.zeros_like(prev_b)),
                                    preferred_element_type=F32)
        state_ref[...] = new_state
        y = y + y_off * jnp.exp(acs_e)
        y = y * _silu(z_ref[rs, :].astype(F32))
        ysq = y * y
        in0 = lane_in < grp_w
        ms0 = jnp.sum(jnp.where(in0, ysq, 0.0), axis=1, keepdims=True) * (1.0 / grp_w)
        ms1 = jnp.sum(jnp.where(in0, 0.0, ysq), axis=1, keepdims=True) * (1.0 / grp_w)
        r = jnp.where(in0, lax.rsqrt(ms0 + EPS), lax.rsqrt(ms1 + EPS))
        o_ref[rs, :] = ((y * r) * gain_ref[...]).astype(o_ref.dtype)


def _ssd(z, xbc, dt, cw, cb, dtb, alog, dvec, gain, tri):
    B, S, _ = z.shape
    lc = SSD_LC
    tok = lambda w: pl.BlockSpec((None, lc, w), lambda b, c: (b, c, 0))
    consts = (cw, cb, dtb, alog, dvec, gain, tri)
    return pl.pallas_call(
        _ssd_kernel,
        grid=(B, S // lc),
        in_specs=[tok(SSD_INNER), tok(SSD_XBC), tok(LANES)] + [_const_spec(a.shape) for a in consts],
        out_specs=tok(SSD_INNER),
        out_shape=jax.ShapeDtypeStruct((B, S, SSD_INNER), BF16),
        scratch_shapes=[pltpu.VMEM((SSD_N, SSD_INNER), F32), pltpu.VMEM((SUBLANES, SSD_XBC), F32)],
        compiler_params=_cparams(("arbitrary", "arbitrary")),
        name="ssd",
    )(z, xbc, dt, *consts)


def _rms(x, gain):
    ms = jnp.mean(x * x, axis=-1, keepdims=True)
    return (x * lax.rsqrt(ms + EPS)) * gain


def _out_ffn_kernel(x_ref, od_ref, ol_ref, os_ref, wo_d_ref, wo_l_ref, wo_s_ref, g_mix_ref, g_pre_ref,
                    up_ref, cw_ref, cb_ref, down_ref, g_post_ref, o_ref, tail_ref, *, tiles_per_seq):
    t = pl.program_id(0)
    tm = x_ref.shape[0]

    @pl.when(t % tiles_per_seq == 0)
    def _():
        tail_ref[...] = jnp.zeros(tail_ref.shape, F32)

    mix = (jnp.dot(od_ref[...], wo_d_ref[...], preferred_element_type=F32)
           + jnp.dot(ol_ref[...], wo_l_ref[...], preferred_element_type=F32)
           + jnp.dot(os_ref[...], wo_s_ref[...], preferred_element_type=F32))
    x1 = x_ref[...] + _rms(mix, g_mix_ref[...])
    h = _rms(x1, g_pre_ref[...]).astype(BF16)

    def up(ci):
        return [jnp.dot(h, up_ref[:, off + ci * FF_CHUNK:off + (ci + 1) * FF_CHUNK], preferred_element_type=F32)
                for off in (0, D_FF)]

    n_chunks = D_FF // FF_CHUNK
    f_acc = jnp.zeros((tm, D_MODEL), F32)
    pending = [up(ci) for ci in range(FF_AHEAD)]
    for ci in range(n_chunks):
        u_pair = pending.pop(0)
        if ci + FF_AHEAD < n_chunks:
            pending.append(up(ci + FF_AHEAD))
        halves = []
        for u, off in zip(u_pair, (0, D_FF)):
            cs = slice(off + ci * FF_CHUNK, off + (ci + 1) * FF_CHUNK)
            tail8 = tail_ref[:, cs]
            cw = cw_ref[:, cs]
            conv = u * cw[FFN_CONV - 1:FFN_CONV, :] + cb_ref[:, cs]
            for kk in range(FFN_CONV - 1):
                conv = conv + _shift_rows(u, tail8, FFN_CONV - 1 - kk) * cw[kk:kk + 1, :]
            tail_ref[:, cs] = u[tm - SUBLANES:tm, :]
            halves.append(conv)
        f = (_silu(halves[0]) * halves[1]).astype(BF16)
        f_acc = f_acc + jnp.dot(f, down_ref[ci * FF_CHUNK:(ci + 1) * FF_CHUNK, :], preferred_element_type=F32)
    o_ref[...] = x1 + _rms(f_acc, g_post_ref[...])


def _out_ffn(x2d, od, ol, os_, wo_d, wo_l, wo_s, g_mix, g_pre, up, cw, cb, down, g_post, tiles_per_seq):
    T, D = x2d.shape
    tm = TM_FFN
    tok = lambda w: pl.BlockSpec((tm, w), lambda t: (t, 0))
    consts = (wo_d, wo_l, wo_s, g_mix, g_pre, up, cw, cb, down, g_post)
    return pl.pallas_call(
        functools.partial(_out_ffn_kernel, tiles_per_seq=tiles_per_seq),
        grid=(T // tm,),
        in_specs=[tok(D), tok(DIFF_W), tok(DIL_W), tok(SSD_INNER)]
                 + [pl.BlockSpec(a.shape, lambda t: (0, 0), pipeline_mode=pl.Buffered(1)) for a in consts],
        out_specs=tok(D),
        out_shape=jax.ShapeDtypeStruct((T, D), F32),
        scratch_shapes=[pltpu.VMEM((SUBLANES, 2 * D_FF), F32)],
        compiler_params=_cparams(("arbitrary",)),
        name="out_ffn",
    )(x2d, od, ol, os_, *consts)


def _rope_tables(S, half):
    inv_freq = jnp.exp(-math.log(ROPE_THETA) * jnp.arange(half, dtype=F32) / half)
    ang = jnp.arange(S, dtype=F32)[:, None] * inv_freq[None, :]
    cos, sin = jnp.cos(ang), jnp.sin(ang)
    zero = jnp.zeros_like(sin)
    reps = LANES // (2 * half)
    cos_t = jnp.tile(jnp.concatenate([cos, cos], axis=1), (1, reps))
    sin_up = jnp.tile(jnp.concatenate([-sin, zero], axis=1), (1, reps))
    sin_dn = jnp.tile(jnp.concatenate([zero, sin], axis=1), (1, reps))
    return cos_t, sin_up, sin_dn


def _rope_tables_t(S, half):
    inv_freq = jnp.exp(-math.log(ROPE_THETA) * jnp.arange(half, dtype=F32) / half)
    ang = jnp.arange(S, dtype=F32)[:, None] * inv_freq[None, :]
    return jnp.cos(ang).T, jnp.sin(ang).T


def kernel(x, pre_mix_norm, w_in, diff_lambda, diff_head_norm, ssd_conv_w, ssd_conv_b, ssd_dt_bias, ssd_A_log,
           ssd_D, ssd_norm, w_out, post_mix_norm, pre_ffn_norm, ffn_up, ffn_conv_w, ffn_conv_b, ffn_down,
           post_ffn_norm):
    B, S, D = x.shape
    depth = w_in.shape[0]
    assert D == D_MODEL and S % DIL_CH == 0 and S % TM_IN == 0 and S % BQ == 0 and S % TM_FFN == 0
    tables = _rope_tables_t(S, DIFF_QK // 2) + _rope_tables(S, DIFF_QK // 2) + _rope_tables(S, DIL_DH // 2)
    tri = (lax.broadcasted_iota(jnp.int32, (SSD_CHUNK, SSD_CHUNK), 1)
           <= lax.broadcasted_iota(jnp.int32, (SSD_CHUNK, SSD_CHUNK), 0)).astype(BF16)
    offs = [0]
    for sz in IN_SIZES:
        offs.append(offs[-1] + sz)

    def pad_lanes(v):
        return jnp.pad(v, ((0, 0), (0, LANES - v.shape[1])))

    for layer in range(depth):
        lambda_init = 0.8 - 0.6 * math.exp(-0.3 * layer)
        wl = w_in[layer].astype(BF16)
        w_a = jnp.concatenate([wl[:, offs[1]:offs[2]], wl[:, offs[3]:offs[5]]], axis=1)
        w_b = jnp.pad(wl[:, offs[5]:], ((0, 0), (0, LANES - SSD_HEADS)))
        w_t = jnp.concatenate([wl[:, offs[0]:offs[1]], wl[:, offs[2]:offs[3]]], axis=1).T
        ws = [w_t, w_a, w_b]
        dqt, dk, dvt, lq, lk, lv, z, xbc, dt = _in_proj(x, pre_mix_norm[layer][None, :], tables, ws)

        o_diff = _diff_attn(dqt, dk, dvt, diff_lambda[layer],
                            jnp.broadcast_to(diff_head_norm[layer][:, None], (DIFF_V, LANES)), lambda_init)
        o_dil = _dil_attn(lq, lk, lv)
        o_ssd = _ssd(z, xbc, dt, ssd_conv_w[layer], ssd_conv_b[layer][None, :],
                     pad_lanes(ssd_dt_bias[layer][None, :]), pad_lanes(ssd_A_log[layer][None, :]),
                     jnp.repeat(ssd_D[layer], SSD_P)[None, :], ssd_norm[layer][None, :], tri)

        wo = w_out[layer].astype(BF16)
        x2d = _out_ffn(
            x.reshape(B * S, D), o_diff.reshape(B * S, DIFF_W), o_dil.reshape(B * S, DIL_W),
            o_ssd.reshape(B * S, SSD_INNER),
            wo[0:DIFF_W], wo[DIFF_W:DIFF_W + DIL_W], wo[DIFF_W + DIL_W:],
            post_mix_norm[layer][None, :], pre_ffn_norm[layer][None, :],
            ffn_up[layer].astype(BF16), ffn_conv_w[layer], ffn_conv_b[layer][None, :],
            ffn_down[layer].astype(BF16), post_ffn_norm[layer][None, :], S // TM_FFN)
        x = x2d.reshape(B, S, D)
    return x
```
